```python
import math
import jax, jax.numpy as jnp
from jax import lax
import numpy as np

D_MODEL = 1024
BATCH = 1
SEQ = 16384
DEPTH = 2
DEC_BATCH = 32
DEC_SEQ = 8
PAST_LEN = 16384
PAGE_SIZE = 128

HEAD_DIM = 64
H_A = 4
H_B = 4
H_C = 4
W_A = H_A * HEAD_DIM
W_B = H_B * HEAD_DIM
W_C = H_C * 2 * HEAD_DIM
SPLIT_SIZES = (W_A, W_A, W_A, W_B, W_B, W_B, H_B, W_C, W_C, W_C)
IN_COLS = 3 * W_A + 3 * W_B + H_B + 3 * W_C
N_BRANCH = 3
Q_BLOCK = 128
N_GROUPS = 4
EXPERTS_PER_GROUP = 8
N_EXPERTS = N_GROUPS * EXPERTS_PER_GROUP
TOP_K_INNER = 2
D_EXPERT = 128
FORGET_BIAS_INIT = 4.0
NORM_EPS = 1e-6
NEG = -1e30
ALIBI_SLOPES = tuple(2.0 ** (-8.0 * (h + 1) / H_C) for h in range(H_C))

kernel_name = "hybrid_stickbreak_fox_diffattn_hmoe_step"


def _rms(x, g):
    x32 = x.astype(jnp.float32)
    y = x32 * lax.rsqrt(jnp.mean(x32 * x32, axis=-1, keepdims=True) + NORM_EPS)
    return (y * g.astype(jnp.float32)).astype(x.dtype)


def _attn_scores(q, k_parts, eq):
    q32 = q.astype(jnp.float32)
    return jnp.concatenate([jnp.einsum(eq, q32, k.astype(jnp.float32)) for k in k_parts], axis=-1)


def _attn_values(w, v_parts, eq):
    outs = []
    off = 0
    for v in v_parts:
        n = v.shape[1]
        outs.append(jnp.einsum(eq, w[..., off:off + n], v.astype(jnp.float32)))
        off += n
    out = outs[0]
    for o in outs[1:]:
        out = out + o
    return out


def _stick_breaking(q, k_parts, v_parts, qpos, kpos):
    z = _attn_scores(q, k_parts, 'bqhd,bkhd->bhqk') * (HEAD_DIM ** -0.5)
    mask = kpos[None, :] < qpos[:, None]
    log_beta = jax.nn.log_sigmoid(z)
    log_stay = jnp.where(mask, log_beta - z, 0.0)
    later = lax.cumsum(log_stay, axis=3, reverse=True) - log_stay
    w = jnp.where(mask, jnp.exp(log_beta + later), 0.0)
    return _attn_values(w, v_parts, 'bhqk,bkhd->bqhd')


def _forgetting(q, k_parts, v_parts, cq, ck, qpos, kpos):
    s = _attn_scores(q, k_parts, 'bqhd,bkhd->bhqk') * (HEAD_DIM ** -0.5)
    s = s + (jnp.swapaxes(cq, 1, 2)[:, :, :, None] - jnp.swapaxes(ck, 1, 2)[:, :, None, :])
    mask = kpos[None, :] <= qpos[:, None]
    p = jax.nn.softmax(jnp.where(mask, s, NEG), axis=-1)
    return _attn_values(p, v_parts, 'bhqk,bkhd->bqhd')


def _differential(q, k_parts, v_parts, lam, qpos, kpos):
    s = _attn_scores(q, k_parts, 'bqhcd,bkhcd->bhcqk') * (HEAD_DIM ** -0.5)
    dist = (qpos[:, None] - kpos[None, :]).astype(jnp.float32)
    slopes = jnp.asarray(ALIBI_SLOPES, dtype=jnp.float32)
    s = s - slopes[None, :, None, None, None] * dist
    mask = kpos[None, :] <= qpos[:, None]
    p = jax.nn.softmax(jnp.where(mask, s, NEG), axis=-1)
    a = p[:, :, 0] - lam * p[:, :, 1]
    return _attn_values(a, v_parts, 'bhqk,bkhe->bqhe')


def _mixer_inputs(xn, p):
    B_, T_ = xn.shape[:2]
    u = xn @ p['w_in']
    cuts = [int(c) for c in np.cumsum(SPLIT_SIZES)[:-1]]
    qa, ka, va, qb, kb, vb, fb, qc, kc, vc = jnp.split(u, cuts, axis=-1)
    hd = (B_, T_, H_A, HEAD_DIM)
    qa, ka, va = qa.reshape(hd), ka.reshape(hd), va.reshape(hd)
    hb = (B_, T_, H_B, HEAD_DIM)
    qb = _rms(qb.reshape(hb), p['qn_b_g'])
    kb = _rms(kb.reshape(hb), p['kn_b_g'])
    vb = vb.reshape(hb)
    logf = jax.nn.log_sigmoid(fb.astype(jnp.float32) + p['b_forget'].astype(jnp.float32))
    hc = (B_, T_, H_C, 2, HEAD_DIM)
    qc = _rms(qc.reshape(hc), p['qn_c_g'])
    kc = _rms(kc.reshape(hc), p['kn_c_g'])
    vc = vc.reshape(B_, T_, H_C, 2 * HEAD_DIM)
    return (qa, ka, va, qb, kb, vb, logf, qc, kc, vc)


def _prompt_mix(mi, lam):
    qa, ka, va, qb, kb, vb, logf, qc, kc, vc = mi
    S = qa.shape[1]
    nb = S // Q_BLOCK
    pos = jnp.arange(S, dtype=jnp.int32)
    cb = lax.cumsum(logf, axis=1)

    def block(i):
        st = i * Q_BLOCK
        sl = lambda t: lax.dynamic_slice_in_dim(t, st, Q_BLOCK, axis=1)
        qp = lax.dynamic_slice_in_dim(pos, st, Q_BLOCK)
        oa = _stick_breaking(sl(qa), (ka,), (va,), qp, pos)
        ob = _forgetting(sl(qb), (kb,), (vb,), sl(cb), cb, qp, pos)
        oc = _differential(sl(qc), (kc,), (vc,), lam, qp, pos)
        return oa, ob, oc

    oa, ob, oc = lax.map(block, jnp.arange(nb, dtype=jnp.int32))
    unblock = lambda o: jnp.moveaxis(o, 0, 1).reshape((o.shape[1], S) + o.shape[3:])
    return unblock(oa), unblock(ob), unblock(oc)


def _gather_pages(pool, l, page_table):
    g = pool[l, page_table]
    return g.reshape((g.shape[0], g.shape[1] * g.shape[2]) + g.shape[3:])


def _sample_mix(mi, lam, cache_a_kv, cache_b_kv, cache_b_logf, cache_c_kv, page_table, l):
    qa, ka, va, qb, kb, vb, logf, qc, kc, vc = mi
    T = qa.shape[1]
    P = page_table.shape[1] * cache_a_kv.shape[2]
    qpos = P + jnp.arange(T, dtype=jnp.int32)
    kpos = jnp.arange(P + T, dtype=jnp.int32)
    pa = _gather_pages(cache_a_kv, l, page_table)
    oa = _stick_breaking(qa, (pa[:, :, 0], ka), (pa[:, :, 1], va), qpos, kpos)
    pb = _gather_pages(cache_b_kv, l, page_table)
    past_logf = _gather_pages(cache_b_logf, l, page_table).astype(jnp.float32)
    c_all = lax.cumsum(jnp.concatenate([past_logf, logf], axis=1), axis=1)
    ob = _forgetting(qb, (pb[:, :, 0], kb), (pb[:, :, 1], vb), c_all[:, P:], c_all, qpos, kpos)
    pc = _gather_pages(cache_c_kv, l, page_table)
    kcp = pc[:, :, 0].reshape(pc.shape[:2] + (H_C, 2, HEAD_DIM))
    oc = _differential(qc, (kcp, kc), (pc[:, :, 1], vc), lam, qpos, kpos)
    return oa, ob, oc


def _merge(xn, oa, ob, oc, p):
    B_, T_ = xn.shape[:2]
    dt = xn.dtype
    ya = oa.reshape(B_, T_, W_A).astype(dt) @ p['w_br_a']
    yb = ob.reshape(B_, T_, W_B).astype(dt) @ p['w_br_b']
    yc = oc.reshape(B_, T_, W_C).astype(dt) @ p['w_br_c']
    g = jax.nn.sigmoid((xn @ p['w_gate'] + p['b_gate']).astype(jnp.float32))
    g = g.reshape(B_, T_, N_BRANCH, D_MODEL)
    merged = (g[..., 0, :] * ya + g[..., 1, :] * yb + g[..., 2, :] * yc).astype(dt)
    return merged @ p['w_out']


def _hier_moe(xn, p):
    shp = xn.shape
    x2 = xn.reshape(-1, D_MODEL)
    n = x2.shape[0]
    gl = (x2 @ p['w_router_grp'] + p['b_router_grp']).astype(jnp.float32)
    gp = jax.nn.softmax(gl, axis=-1)
    g_idx = jnp.argmax(gl, axis=-1)
    p_g = jnp.take_along_axis(gp, g_idx[:, None], axis=-1)
    el = (x2 @ p['w_router_exp'] + p['b_router_exp']).astype(jnp.float32)
    el = el.reshape(n, N_GROUPS, EXPERTS_PER_GROUP)
    el = jnp.take_along_axis(el, g_idx[:, None, None], axis=1)[:, 0]
    ep = jax.nn.softmax(el, axis=-1)
    top_p, top_i = lax.top_k(ep, TOP_K_INNER)
    w = p_g * top_p / jnp.sum(top_p, axis=-1, keepdims=True)
    e_idx = g_idx[:, None] * EXPERTS_PER_GROUP + top_i
    gates = jnp.sum(jax.nn.one_hot(e_idx, N_EXPERTS, dtype=jnp.float32) * w[..., None], axis=1)
    h = jax.nn.silu(jnp.einsum('nd,edf->nef', x2, p['w_exp_gate'])) * jnp.einsum('nd,edf->nef', x2, p['w_exp_up'])
    h = h * gates[..., None].astype(h.dtype)
    y = jnp.einsum('nef,efd->nd', h, p['w_exp_down'])
    return y.reshape(shp)


def _cache_rows(mi):
    qa, ka, va, qb, kb, vb, logf, qc, kc, vc = mi
    a = jnp.stack([ka, va], axis=2)
    b = jnp.stack([kb, vb], axis=2)
    lf = logf.astype(kb.dtype)
    c = jnp.stack([kc.reshape(kc.shape[:3] + (2 * HEAD_DIM,)), vc], axis=2)
    return a, b, lf, c


def _layer(h, mix_fn, p):
    xn = _rms(h, p['norm1_g'])
    mi = _mixer_inputs(xn, p)
    oa, ob, oc = mix_fn(mi, p['lam'])
    oc = _rms(oc, p['subln_c_g']) * (1.0 - p['lam_init'])
    h = h + _merge(xn, oa, ob, oc, p)
    h = h + _hier_moe(_rms(h, p['norm2_g']), p)
    return h, _cache_rows(mi)


def setup_inputs(seed: int = 0) -> dict:
    key = jax.random.key(seed)
    keys = list(jax.random.split(key, 48))
    f32 = jnp.float32

    def nrm(shape, scale=1.0):
        return jax.random.normal(keys.pop(), shape, f32) * scale

    n_pages = PAST_LEN // PAGE_SIZE
    n_used = DEC_BATCH * n_pages
    n_pool = n_used + (n_used + 3) // 4
    page_table = jax.random.permutation(keys.pop(), n_pool)[:n_used].reshape(DEC_BATCH, n_pages).astype(jnp.int32)

    x_prompt = nrm((BATCH, SEQ, D_MODEL))
    x_sample = nrm((DEC_BATCH, DEC_SEQ, D_MODEL))
    cache_a_kv = nrm((DEPTH, n_pool, PAGE_SIZE, 2, H_A, HEAD_DIM))
    cache_b_kv = nrm((DEPTH, n_pool, PAGE_SIZE, 2, H_B, HEAD_DIM))
    cache_b_logf = jax.nn.log_sigmoid(FORGET_BIAS_INIT + nrm((DEPTH, n_pool, PAGE_SIZE, H_B)))
    cache_c_kv = nrm((DEPTH, n_pool, PAGE_SIZE, 2, H_C, 2 * HEAD_DIM))

    gain = lambda shape: 1.0 + nrm(shape, 0.02)
    d_inv = D_MODEL ** -0.5
    return {
        'x_prompt': x_prompt,
        'x_sample': x_sample,
        'cache_a_kv': cache_a_kv,
        'cache_b_kv': cache_b_kv,
        'cache_b_logf': cache_b_logf,
        'cache_c_kv': cache_c_kv,
        'page_table': page_table,
        'norm1_g': gain((DEPTH, D_MODEL)),
        'w_in': nrm((DEPTH, D_MODEL, IN_COLS), d_inv),
        'b_forget': FORGET_BIAS_INIT + nrm((DEPTH, H_B), 0.1),
        'qn_b_g': gain((DEPTH, HEAD_DIM)),
        'kn_b_g': gain((DEPTH, HEAD_DIM)),
        'qn_c_g': gain((DEPTH, HEAD_DIM)),
        'kn_c_g': gain((DEPTH, HEAD_DIM)),
        'lam_q1': nrm((DEPTH, HEAD_DIM), 0.1),
        'lam_k1': nrm((DEPTH, HEAD_DIM), 0.1),
        'lam_q2': nrm((DEPTH, HEAD_DIM), 0.1),
        'lam_k2': nrm((DEPTH, HEAD_DIM), 0.1),
        'subln_c_g': gain((DEPTH, 2 * HEAD_DIM)),
        'w_br_a': nrm((DEPTH, W_A, D_MODEL), W_A ** -0.5),
        'w_br_b': nrm((DEPTH, W_B, D_MODEL), W_B ** -0.5),
        'w_br_c': nrm((DEPTH, W_C, D_MODEL), W_C ** -0.5),
        'w_gate': nrm((DEPTH, D_MODEL, N_BRANCH * D_MODEL), d_inv),
        'b_gate': nrm((DEPTH, N_BRANCH * D_MODEL), 0.01),
        'w_out': nrm((DEPTH, D_MODEL, D_MODEL), d_inv),
        'norm2_g': gain((DEPTH, D_MODEL)),
        'w_router_grp': nrm((DEPTH, D_MODEL, N_GROUPS), d_inv),
        'b_router_grp': nrm((DEPTH, N_GROUPS), 0.01),
        'w_router_exp': nrm((DEPTH, D_MODEL, N_EXPERTS), d_inv),
        'b_router_exp': nrm((DEPTH, N_EXPERTS), 0.01),
        'w_exp_gate': nrm((DEPTH, N_EXPERTS, D_MODEL, D_EXPERT), d_inv),
        'w_exp_up': nrm((DEPTH, N_EXPERTS, D_MODEL, D_EXPERT), d_inv),
        'w_exp_down': nrm((DEPTH, N_EXPERTS, D_EXPERT, D_MODEL), D_EXPERT ** -0.5),
    }


def reference(x_prompt, x_sample, cache_a_kv, cache_b_kv, cache_b_logf, cache_c_kv, page_table,
              norm1_g, w_in, b_forget, qn_b_g, kn_b_g, qn_c_g, kn_c_g,
              lam_q1, lam_k1, lam_q2, lam_k2, subln_c_g, w_br_a, w_br_b, w_br_c,
              w_gate, b_gate, w_out, norm2_g, w_router_grp, b_router_grp,
              w_router_exp, b_router_exp, w_exp_gate, w_exp_up, w_exp_down):
    hp = x_prompt
    hs = x_sample
    rows_p = ([], [], [], [])
    rows_s = ([], [], [], [])
    for l in range(DEPTH):
        lam_init = 0.8 - 0.6 * math.exp(-0.3 * l)
        lam = (jnp.exp(jnp.sum(lam_q1[l].astype(jnp.float32) * lam_k1[l].astype(jnp.float32)))
               - jnp.exp(jnp.sum(lam_q2[l].astype(jnp.float32) * lam_k2[l].astype(jnp.float32)))
               + lam_init)
        p = {
            'norm1_g': norm1_g[l], 'w_in': w_in[l], 'b_forget': b_forget[l],
            'qn_b_g': qn_b_g[l], 'kn_b_g': kn_b_g[l], 'qn_c_g': qn_c_g[l], 'kn_c_g': kn_c_g[l],
            'lam': lam, 'lam_init': lam_init, 'subln_c_g': subln_c_g[l],
            'w_br_a': w_br_a[l], 'w_br_b': w_br_b[l], 'w_br_c': w_br_c[l],
            'w_gate': w_gate[l], 'b_gate': b_gate[l], 'w_out': w_out[l], 'norm2_g': norm2_g[l],
            'w_router_grp': w_router_grp[l], 'b_router_grp': b_router_grp[l],
            'w_router_exp': w_router_exp[l], 'b_router_exp': b_router_exp[l],
            'w_exp_gate': w_exp_gate[l], 'w_exp_up': w_exp_up[l], 'w_exp_down': w_exp_down[l],
        }
        hp, rp = _layer(hp, _prompt_mix, p)
        sample_fn = lambda mi, lm, l=l: _sample_mix(mi, lm, cache_a_kv, cache_b_kv, cache_b_logf,
                                                    cache_c_kv, page_table, l)
        hs, rs = _layer(hs, sample_fn, p)
        for i in range(4):
            rows_p[i].append(rp[i])
            rows_s[i].append(rs[i])
    new_a_kv_p = jnp.stack(rows_p[0])
    new_b_kv_p = jnp.stack(rows_p[1])
    new_b_logf_p = jnp.stack(rows_p[2])
    new_c_kv_p = jnp.stack(rows_p[3])
    new_a_kv_s = jnp.stack(rows_s[0])
    new_b_kv_s = jnp.stack(rows_s[1])
    new_b_logf_s = jnp.stack(rows_s[2])
    new_c_kv_s = jnp.stack(rows_s[3])
    return (hp, hs, new_a_kv_p, new_b_kv_p, new_b_logf_p, new_c_kv_p,
            new_a_kv_s, new_b_kv_s, new_b_logf_s, new_c_kv_s)
```

```python
import functools
import math

import jax
import jax.numpy as jnp
from jax import lax
from jax.experimental import pallas as pl
from jax.experimental.pallas import tpu as pltpu

F32 = jnp.float32
BF16 = jnp.bfloat16

HEAD_DIM = 64
HEAD_SHIFT = 6
N_HEADS = 4
PAIR = 2 * HEAD_DIM
N_GROUPS = 4
EXPERTS_PER_GROUP = 8
GROUP_SHIFT = 3
D_EXPERT = 128
NORM_EPS = 1e-6
NEG = -1e30
LANES = 128
GATE_LANE0 = 8
STICK_SKIP = 110.0
ALIBI_SLOPES = tuple(2.0 ** (-8.0 * (h + 1) / N_HEADS) for h in range(N_HEADS))
VMEM_LIMIT = 56 * 1024 * 1024


def _dot(a, b):
    return jnp.dot(a, b, preferred_element_type=F32)


def _dot_nt(a, b):
    return lax.dot_general(a, b, (((1,), (1,)), ((), ())), preferred_element_type=F32)


def _split(x):
    hi = x.astype(BF16)
    lo = (x - hi.astype(F32)).astype(BF16)
    return hi, lo


def _log_sigmoid(x):
    return jnp.minimum(x, 0.0) - jnp.log1p(jnp.exp(-jnp.abs(x)))


def _rms_rows(x, g):
    return x * lax.rsqrt(jnp.mean(x * x, axis=-1, keepdims=True) + NORM_EPS) * g


def _params(sem):
    return pltpu.CompilerParams(dimension_semantics=sem, vmem_limit_bytes=VMEM_LIMIT)


def _proj_kernel(h_ref, g1_ref, wm_ref, wf_ref, bf_ref, gb_ref, gc_ref, seg_ref, tri_ref,
                 arow_ref, brow_ref, crow_ref, lf_ref, cs_ref, abf_ref, bbf_ref, cbf_ref,
                 carry_ref, *, tiles_per_seq):
    i = pl.program_id(0)
    tm = h_ref.shape[0]
    xb = _rms_rows(h_ref[...], g1_ref[...]).astype(BF16)
    seg = seg_ref[...]
    wa = N_HEADS * HEAD_DIM
    wc = 2 * wa

    def headnorm(u, g):
        hi, lo = _split(u * u)
        ms = _dot(hi, seg) + _dot(lo, seg)
        return u * lax.rsqrt(ms + NORM_EPS) * g

    ua = _dot(xb, wm_ref[:, 0:3 * wa])
    arow_ref[...] = ua[:, wa:3 * wa]
    abf_ref[:, 0:wa] = (ua[:, 0:wa] * 0.125).astype(BF16)
    abf_ref[:, wa:3 * wa] = ua[:, wa:3 * wa].astype(BF16)

    ub = _dot(xb, wm_ref[:, 3 * wa:6 * wa])
    qb = headnorm(ub[:, 0:wa], gb_ref[0:1, :])
    kb = headnorm(ub[:, wa:2 * wa], gb_ref[1:2, :])
    vb = ub[:, 2 * wa:3 * wa]
    brow_ref[:, 0:wa] = kb
    brow_ref[:, wa:2 * wa] = vb
    bbf_ref[:, 0:wa] = (qb * 0.125).astype(BF16)
    bbf_ref[:, wa:2 * wa] = kb.astype(BF16)
    bbf_ref[:, 2 * wa:3 * wa] = vb.astype(BF16)

    c0 = 6 * wa
    uc = _dot(xb, wm_ref[:, c0:c0 + 3 * wc])
    for half in range(2):
        sl = slice(half * wa, (half + 1) * wa)
        qc = headnorm(uc[:, half * wa:(half + 1) * wa], gc_ref[0:1, sl])
        kc = headnorm(uc[:, wc + half * wa:wc + (half + 1) * wa], gc_ref[1:2, sl])
        cbf_ref[:, half * wa:(half + 1) * wa] = (qc * 0.125).astype(BF16)
        cbf_ref[:, wc + half * wa:wc + (half + 1) * wa] = kc.astype(BF16)
        crow_ref[:, half * wa:(half + 1) * wa] = kc
    vc = uc[:, 2 * wc:3 * wc]
    crow_ref[:, wc:2 * wc] = vc
    cbf_ref[:, 2 * wc:3 * wc] = vc.astype(BF16)

    lf = _log_sigmoid(_dot(xb, wf_ref[...]) + bf_ref[...])
    lf_ref[...] = lf

    @pl.when(i % tiles_per_seq == 0)
    def _():
        carry_ref[...] = jnp.zeros_like(carry_ref)

    hi, lo = _split(lf)
    tri = tri_ref[...]
    cs = _dot(tri, hi) + _dot(tri, lo) + carry_ref[...]
    cs_ref[...] = cs
    carry_ref[...] = cs[tm - 1:tm, :]


def _proj(h2d, tri, tiles_per_seq, g1, wm, wf, bfp, gb, gc, seg):
    n, d = h2d.shape
    tm = tri.shape[0]
    wa = N_HEADS * HEAD_DIM
    const = lambda shape: pl.BlockSpec(shape, lambda i: (0,) * len(shape))
    row = lambda w: pl.BlockSpec((tm, w), lambda i: (i, 0))
    out_shapes = [
        jax.ShapeDtypeStruct((n, 2 * wa), F32), jax.ShapeDtypeStruct((n, 2 * wa), F32),
        jax.ShapeDtypeStruct((n, 4 * wa), F32), jax.ShapeDtypeStruct((n, LANES), F32),
        jax.ShapeDtypeStruct((n, LANES), F32), jax.ShapeDtypeStruct((n, 3 * wa), BF16),
        jax.ShapeDtypeStruct((n, 3 * wa), BF16), jax.ShapeDtypeStruct((n, 6 * wa), BF16),
    ]
    return pl.pallas_call(
        functools.partial(_proj_kernel, tiles_per_seq=tiles_per_seq),
        grid=(n // tm,),
        in_specs=[row(d), const((1, d)), const(wm.shape), const(wf.shape), const((1, LANES)),
                  const(gb.shape), const(gc.shape), const(seg.shape), const(tri.shape)],
        out_specs=[row(2 * wa), row(2 * wa), row(4 * wa), row(LANES), row(LANES),
                   row(3 * wa), row(3 * wa), row(6 * wa)],
        out_shape=out_shapes,
        scratch_shapes=[pltpu.VMEM((1, LANES), F32)],
        compiler_params=_params(("arbitrary",)),
        name="proj",
    )(h2d, g1, wm, wf, bfp, gb, gc, seg, tri)


def _head_mask(hh):
    lane = lax.broadcasted_iota(jnp.int32, (1, PAIR), 1)
    return (lane >> HEAD_SHIFT) == hh


def _store_head(o_ref, hh, hmask, res):
    @pl.when(hh == 0)
    def _():
        o_ref[...] = jnp.where(hmask, res, 0.0).astype(o_ref.dtype)

    @pl.when(hh != 0)
    def _():
        o_ref[...] = jnp.where(hmask, res.astype(o_ref.dtype), o_ref[...])


def _stick_kernel(q_ref, k_ref, v_ref, ltri_ref, o_ref):
    i = pl.program_id(2)
    hh = pl.program_id(3)
    tq = q_ref.shape[0]
    hmask = _head_mask(hh)
    q = q_ref[...]
    qm = jnp.where(hmask, q, jnp.zeros_like(q))
    ltri = ltri_ref[...]
    row = lax.broadcasted_iota(jnp.int32, (tq, tq), 0)
    col = lax.broadcasted_iota(jnp.int32, (tq, tq), 1)

    def block(j, run, acc, diag):
        off = pl.multiple_of(j * tq, tq)
        k = k_ref[pl.ds(off, tq), :]
        v = v_ref[pl.ds(off, tq), :]
        z = _dot_nt(qm, k)
        lb = _log_sigmoid(z)
        ls = lb - z
        if diag:
            valid = col < row
            ls = jnp.where(valid, ls, 0.0)
        hi, lo = _split(ls)
        later = _dot(hi, ltri) + _dot(lo, ltri)
        w = jnp.exp(lb + later + run)
        if diag:
            w = jnp.where(valid, w, 0.0)
        acc = acc + _dot(w.astype(BF16), v)
        run = run + jnp.sum(ls, axis=1, keepdims=True)
        return run, acc

    run, acc = block(i, jnp.zeros((tq, 1), F32), jnp.zeros((tq, PAIR), F32), True)

    def cond(c):
        j, run, _ = c
        return jnp.logical_and(j >= 0, jnp.max(run) > -STICK_SKIP)

    def body(c):
        j, run, acc = c
        run, acc = block(j, run, acc, False)
        return j - 1, run, acc

    _, _, acc = lax.while_loop(cond, body, (i - 1, run, acc))
    _store_head(o_ref, hh, hmask, acc)


def _fox_kernel(q_ref, k_ref, v_ref, cq_ref, ck_ref, o_ref):
    pair = pl.program_id(1)
    i = pl.program_id(2)
    hh = pl.program_id(3)
    h = pair * 2 + hh
    tq = q_ref.shape[0]
    hmask = _head_mask(hh)
    q = q_ref[...]
    qm = jnp.where(hmask, q, jnp.zeros_like(q))
    lane = lax.broadcasted_iota(jnp.int32, (1, LANES), 1)
    cq = jnp.sum(jnp.where(lane == h, cq_ref[...], 0.0), axis=1, keepdims=True)
    row = lax.broadcasted_iota(jnp.int32, (tq, tq), 0)
    col = lax.broadcasted_iota(jnp.int32, (tq, tq), 1)

    def block(j, m, l, acc, diag):
        off = pl.multiple_of(j * tq, tq)
        k = k_ref[pl.ds(off, tq), :]
        v = v_ref[pl.ds(off, tq), :]
        ck = ck_ref[pl.ds(h, 1), pl.ds(off, tq)]
        s = _dot_nt(qm, k) + (cq - ck)
        if diag:
            s = jnp.where(col <= row, s, NEG)
        m_new = jnp.maximum(m, jnp.max(s, axis=1, keepdims=True))
        alpha = jnp.exp(m - m_new)
        p = jnp.exp(s - m_new)
        l = alpha * l + jnp.sum(p, axis=1, keepdims=True)
        acc = alpha * acc + _dot(p.astype(BF16), v)
        return m_new, l, acc

    m, l, acc = block(i, jnp.full((tq, 1), NEG, F32), jnp.zeros((tq, 1), F32),
                      jnp.zeros((tq, PAIR), F32), True)
    m, l, acc = lax.fori_loop(0, i, lambda t, c: block(i - 1 - t, *c, False), (m, l, acc))
    _store_head(o_ref, hh, hmask, acc / l)


def _diff_kernel(slope_ref, lam_ref, q_ref, k_ref, v_ref, g_ref, o_ref, *, out_scale):
    h = pl.program_id(1)
    i = pl.program_id(2)
    tq = q_ref.shape[0]
    slope = slope_ref[h]
    lam = lam_ref[0]
    q = q_ref[...]
    qms = [jnp.where(_head_mask(c), q, jnp.zeros_like(q)) for c in range(2)]
    row = lax.broadcasted_iota(jnp.int32, (tq, tq), 0)
    col = lax.broadcasted_iota(jnp.int32, (tq, tq), 1)
    sd0 = slope * (row - col).astype(F32)

    def block(j, carry, diag):
        off = pl.multiple_of(j * tq, tq)
        k = k_ref[pl.ds(off, tq), :]
        v = v_ref[pl.ds(off, tq), :]
        sb = sd0 + slope * ((i - j) * tq).astype(F32)
        new = []
        for c in range(2):
            m, l, acc = carry[c]
            s = _dot_nt(qms[c], k) - sb
            if diag:
                s = jnp.where(col <= row, s, NEG)
            m_new = jnp.maximum(m, jnp.max(s, axis=1, keepdims=True))
            alpha = jnp.exp(m - m_new)
            p = jnp.exp(s - m_new)
            l = alpha * l + jnp.sum(p, axis=1, keepdims=True)
            acc = alpha * acc + _dot(p.astype(BF16), v)
            new.append((m_new, l, acc))
        return tuple(new)

    init = tuple((jnp.full((tq, 1), NEG, F32), jnp.zeros((tq, 1), F32), jnp.zeros((tq, PAIR), F32))
                 for _ in range(2))
    carry = block(i, init, True)
    carry = lax.fori_loop(0, i, lambda t, c: block(i - 1 - t, c, False), carry)
    (_, l0, a0), (_, l1, a1) = carry
    o = a0 / l0 - lam * (a1 / l1)
    o_ref[...] = (_rms_rows(o, g_ref[...]) * out_scale).astype(o_ref.dtype)


def _prompt_mixers(abf, bbf, cbf, cs, ck, ltri, slopes, lam, gsub, out_scale, tq_a, tq):
    bsz, t, _ = abf.shape
    npair = N_HEADS // 2
    wa = N_HEADS * HEAD_DIM

    def pair_specs(tile):
        qs = pl.BlockSpec((None, tile, PAIR), lambda b, p, i, hh: (b, i, p))
        ks = pl.BlockSpec((None, t, PAIR), lambda b, p, i, hh: (b, 0, npair + p))
        vs = pl.BlockSpec((None, t, PAIR), lambda b, p, i, hh: (b, 0, 2 * npair + p))
        os = pl.BlockSpec((None, tile, PAIR), lambda b, p, i, hh: (b, i, p))
        return qs, ks, vs, os

    sem4 = ("arbitrary",) * 4
    qs, ks, vs, os = pair_specs(tq_a)
    oa = pl.pallas_call(
        _stick_kernel,
        grid=(bsz, npair, t // tq_a, 2),
        in_specs=[qs, ks, vs, pl.BlockSpec(ltri.shape, lambda b, p, i, hh: (0, 0))],
        out_specs=os,
        out_shape=jax.ShapeDtypeStruct((bsz, t, wa), BF16),
        compiler_params=_params(sem4),
        name="stick_prompt",
    )(abf, abf, abf, ltri)

    qs, ks, vs, os = pair_specs(tq)
    ob = pl.pallas_call(
        _fox_kernel,
        grid=(bsz, npair, t // tq, 2),
        in_specs=[qs, ks, vs,
                  pl.BlockSpec((None, tq, LANES), lambda b, p, i, hh: (b, i, 0)),
                  pl.BlockSpec((None, 8, t), lambda b, p, i, hh: (b, 0, 0))],
        out_specs=os,
        out_shape=jax.ShapeDtypeStruct((bsz, t, wa), BF16),
        compiler_params=_params(sem4),
        name="fox_prompt",
    )(bbf, bbf, bbf, cs, ck)

    smem = pl.BlockSpec(memory_space=pltpu.SMEM)
    oc = pl.pallas_call(
        functools.partial(_diff_kernel, out_scale=out_scale),
        grid=(bsz, N_HEADS, t // tq),
        in_specs=[smem, smem,
                  pl.BlockSpec((None, tq, PAIR), lambda b, h, i: (b, i, h)),
                  pl.BlockSpec((None, t, PAIR), lambda b, h, i: (b, 0, N_HEADS + h)),
                  pl.BlockSpec((None, t, PAIR), lambda b, h, i: (b, 0, 2 * N_HEADS + h)),
                  pl.BlockSpec((1, PAIR), lambda b, h, i: (0, 0))],
        out_specs=pl.BlockSpec((None, tq, PAIR), lambda b, h, i: (b, i, h)),
        out_shape=jax.ShapeDtypeStruct((bsz, t, 2 * wa), BF16),
        compiler_params=_params(("arbitrary",) * 3),
        name="diff_prompt",
    )(slopes, lam, cbf, cbf, cbf, gsub)
    return oa, ob, oc


def _sample_kernel(pt_ref, lam_ref, qa_ref, qb_ref, qc_ref, csq_ref, csk_ref, ltri_ref, g_ref, *rest,
                   pages_per_step, n_pages, out_scale):
    del pt_ref
    gsz = pages_per_step
    page_refs = rest[:4 * gsz]
    oa_ref, ob_ref, oc_ref = rest[4 * gsz:4 * gsz + 3]
    (ra_ref, acca_ref, mb_ref, lb_ref, accb_ref, carry_ref,
     mc_ref, lc_ref, accc_ref) = rest[4 * gsz + 3:]
    j = pl.program_id(1)
    nq = qa_ref.shape[0]
    page = ltri_ref.shape[0]
    wa = N_HEADS * HEAD_DIM
    ra_rows = N_HEADS * nq
    rc_rows = 2 * nq
    lam = lam_ref[0]
    ltri = ltri_ref[...]

    lane_a = lax.broadcasted_iota(jnp.int32, (ra_rows, wa), 1)
    row_a = lax.broadcasted_iota(jnp.int32, (ra_rows, wa), 0)
    nq_shift = nq.bit_length() - 1
    blockdiag = (lane_a >> HEAD_SHIFT) == (row_a >> nq_shift)
    lane_k = lax.broadcasted_iota(jnp.int32, (ra_rows, page), 1)
    qi_a = lax.broadcasted_iota(jnp.int32, (ra_rows, page), 0) & (nq - 1)
    lane_kc = lax.broadcasted_iota(jnp.int32, (rc_rows, page), 1)
    row_c = lax.broadcasted_iota(jnp.int32, (rc_rows, page), 0)
    qi_c = row_c & (nq - 1)
    map_c = (lax.broadcasted_iota(jnp.int32, (rc_rows, PAIR), 1) >> HEAD_SHIFT) == \
            (lax.broadcasted_iota(jnp.int32, (rc_rows, PAIR), 0) >> nq_shift)

    def stacked_q(x, reps):
        return jnp.concatenate([x] * reps, axis=0)

    def pad_rows(x):
        return jnp.concatenate([x, jnp.zeros((page - nq, x.shape[1]), x.dtype)], axis=0)

    qa = qa_ref[...]
    qb = qb_ref[...]
    qc = qc_ref[...]
    qma = jnp.where(blockdiag, stacked_q(qa[:, 0:wa], N_HEADS), 0.0).astype(BF16)
    qmb = jnp.where(blockdiag, stacked_q(qb[:, 0:wa], N_HEADS), 0.0).astype(BF16)
    qmc = [jnp.where(map_c, stacked_q(qc[:, h * PAIR:(h + 1) * PAIR], 2), 0.0).astype(BF16)
           for h in range(N_HEADS)]

    def expand_heads(x):
        return jnp.concatenate(
            [jnp.broadcast_to(x[h:h + 1, :], (nq, x.shape[1])) for h in range(N_HEADS)], axis=0)

    lane8 = lax.broadcasted_iota(jnp.int32, (nq, LANES), 1)
    csq = csq_ref[...]
    fnew = jnp.concatenate(
        [jnp.sum(jnp.where(lane8 == h, csq, 0.0), axis=1, keepdims=True) for h in range(N_HEADS)],
        axis=0)

    def stick_update(z, valid, v_dot, run, acc):
        lb = _log_sigmoid(z)
        ls = lb - z
        if valid is not None:
            ls = jnp.where(valid, ls, 0.0)
        hi, lo = _split(ls)
        later = _dot(hi, ltri) + _dot(lo, ltri)
        w = jnp.exp(lb + later + run)
        if valid is not None:
            w = jnp.where(valid, w, 0.0)
        acc = acc + v_dot(w.astype(BF16))
        run = run + jnp.sum(ls, axis=1, keepdims=True)
        return run, acc

    def softmax_update(s, valid, v_dot, m, l, acc):
        if valid is not None:
            s = jnp.where(valid, s, NEG)
        m_new = jnp.maximum(m, jnp.max(s, axis=1, keepdims=True))
        alpha = jnp.exp(m - m_new)
        p = jnp.exp(s - m_new)
        l = alpha * l + jnp.sum(p, axis=1, keepdims=True)
        acc = alpha * acc + v_dot(p.astype(BF16))
        return m_new, l, acc

    @pl.when(j == 0)
    def _():
        ka = pad_rows(qa[:, wa:2 * wa]).astype(BF16)
        va = pad_rows(qa[:, 2 * wa:3 * wa]).astype(BF16)
        run, acc = stick_update(_dot_nt(qma, ka), lane_k < qi_a, lambda w: _dot(w, va),
                                jnp.zeros((ra_rows, 1), F32), jnp.zeros((ra_rows, wa), F32))
        ra_ref[...] = run
        acca_ref[...] = acc

        kb = pad_rows(qb[:, wa:2 * wa]).astype(BF16)
        vb = pad_rows(qb[:, 2 * wa:3 * wa]).astype(BF16)
        bias = fnew - expand_heads(csk_ref[...])
        m, l, acc = softmax_update(_dot_nt(qmb, kb) + bias, lane_k <= qi_a, lambda p: _dot(p, vb),
                                   jnp.full((ra_rows, 1), NEG, F32), jnp.zeros((ra_rows, 1), F32),
                                   jnp.zeros((ra_rows, wa), F32))
        mb_ref[...] = m
        lb_ref[...] = l
        accb_ref[...] = acc
        carry_ref[...] = jnp.zeros_like(carry_ref)

        dist = (qi_c - lane_kc).astype(F32)
        for h in range(N_HEADS):
            kc = pad_rows(qc[:, 2 * wa + h * PAIR:2 * wa + (h + 1) * PAIR]).astype(BF16)
            vc = pad_rows(qc[:, 4 * wa + h * PAIR:4 * wa + (h + 1) * PAIR]).astype(BF16)
            s = _dot_nt(qmc[h], kc) - ALIBI_SLOPES[h] * dist
            m, l, acc = softmax_update(s, lane_kc <= qi_c, lambda p, vc=vc: _dot(p, vc),
                                       jnp.full((rc_rows, 1), NEG, F32), jnp.zeros((rc_rows, 1), F32),
                                       jnp.zeros((rc_rows, PAIR), F32))
            rows = slice(h * rc_rows, (h + 1) * rc_rows)
            mc_ref[rows, :] = m
            lc_ref[rows, :] = l
            accc_ref[rows, :] = acc

    run_a = ra_ref[...]
    acc_a = acca_ref[...]
    m_b = mb_ref[...]
    l_b = lb_ref[...]
    acc_b = accb_ref[...]
    carry = carry_ref[...]
    st_c = [(mc_ref[h * rc_rows:(h + 1) * rc_rows, :], lc_ref[h * rc_rows:(h + 1) * rc_rows, :],
             accc_ref[h * rc_rows:(h + 1) * rc_rows, :]) for h in range(N_HEADS)]
    past_len = n_pages * page

    for g in range(gsz):
        a_ref, b_ref, f_ref, c_ref = page_refs[4 * g:4 * g + 4]
        page_idx = n_pages - 1 - (j * gsz + g)

        kta = a_ref[0].astype(BF16)
        vta = a_ref[1].astype(BF16)
        run_a, acc_a = stick_update(_dot(qma, kta), None, lambda w, vta=vta: _dot_nt(w, vta),
                                    run_a, acc_a)

        lf_r = expand_heads(f_ref[...])
        hi, lo = _split(lf_r)
        sfx_r = _dot(hi, ltri) + _dot(lo, ltri)
        ktb = b_ref[0].astype(BF16)
        vtb = b_ref[1].astype(BF16)
        s = _dot(qmb, ktb) + (sfx_r + (fnew + carry))
        m_b, l_b, acc_b = softmax_update(s, None, lambda p, vtb=vtb: _dot_nt(p, vtb), m_b, l_b, acc_b)
        carry = carry + (sfx_r[:, 0:1] + lf_r[:, 0:1])

        dist = (qi_c - lane_kc + (past_len - page_idx * page)).astype(F32)
        for h in range(N_HEADS):
            kc = c_ref[pl.ds(h, page, stride=2 * N_HEADS), :].astype(BF16)
            vc = c_ref[pl.ds(N_HEADS + h, page, stride=2 * N_HEADS), :].astype(BF16)
            s = _dot_nt(qmc[h], kc) - ALIBI_SLOPES[h] * dist
            st_c[h] = softmax_update(s, None, lambda p, vc=vc: _dot(p, vc), *st_c[h])

    ra_ref[...] = run_a
    acca_ref[...] = acc_a
    mb_ref[...] = m_b
    lb_ref[...] = l_b
    accb_ref[...] = acc_b
    carry_ref[...] = carry
    for h in range(N_HEADS):
        rows = slice(h * rc_rows, (h + 1) * rc_rows)
        mc_ref[rows, :] = st_c[h][0]
        lc_ref[rows, :] = st_c[h][1]
        accc_ref[rows, :] = st_c[h][2]

    @pl.when(j == pl.num_programs(1) - 1)
    def _():
        lane_o = lax.broadcasted_iota(jnp.int32, (nq, wa), 1) >> HEAD_SHIFT
        nb = acc_b / l_b
        oa = jnp.zeros((nq, wa), F32)
        ob = jnp.zeros((nq, wa), F32)
        for h in range(N_HEADS):
            oa = jnp.where(lane_o == h, acc_a[h * nq:(h + 1) * nq, :], oa)
            ob = jnp.where(lane_o == h, nb[h * nq:(h + 1) * nq, :], ob)
        oa_ref[...] = oa
        ob_ref[...] = ob
        for h in range(N_HEADS):
            _, l, acc = st_c[h]
            n = acc / l
            o = n[0:nq, :] - lam * n[nq:2 * nq, :]
            oc_ref[:, h * PAIR:(h + 1) * PAIR] = _rms_rows(o, g_ref[...]) * out_scale


def _sample_mixers(layer, page_table, lam, abf, bbf, cbf, csq, csk, ltri, gsub,
                   pool_a, pool_b, pool_f, pool_c, out_scale, pages_per_step):
    dbsz, nq, _ = abf.shape
    n_pages = page_table.shape[1]
    page = ltri.shape[0]
    wa = N_HEADS * HEAD_DIM
    gsz = pages_per_step
    steps = n_pages // gsz

    def seq_spec(w):
        return pl.BlockSpec((None, nq, w), lambda b, j, pt: (b, 0, 0))

    def page_specs(g):
        def idx(b, j, pt):
            return pt[b, n_pages - 1 - (j * gsz + g)]
        return [
            pl.BlockSpec((None, None, 2, wa, page), lambda b, j, pt: (layer, idx(b, j, pt), 0, 0, 0)),
            pl.BlockSpec((None, None, 2, wa, page), lambda b, j, pt: (layer, idx(b, j, pt), 0, 0, 0)),
            pl.BlockSpec((None, None, N_HEADS, page), lambda b, j, pt: (layer, idx(b, j, pt), 0, 0)),
            pl.BlockSpec((None, None, 2 * N_HEADS * page, PAIR),
                         lambda b, j, pt: (layer, idx(b, j, pt), 0, 0)),
        ]

    in_specs = [pl.BlockSpec(memory_space=pltpu.SMEM), seq_spec(3 * wa), seq_spec(3 * wa), seq_spec(6 * wa),
                seq_spec(LANES), pl.BlockSpec((None, 8, LANES), lambda b, j, pt: (b, 0, 0)),
                pl.BlockSpec(ltri.shape, lambda b, j, pt: (0, 0)),
                pl.BlockSpec((1, PAIR), lambda b, j, pt: (0, 0))]
    pools = []
    for g in range(gsz):
        in_specs += page_specs(g)
        pools += [pool_a, pool_b, pool_f, pool_c]
    ra_rows = N_HEADS * nq
    rc_rows = N_HEADS * 2 * nq
    scratch = [pltpu.VMEM((ra_rows, 1), F32), pltpu.VMEM((ra_rows, wa), F32),
               pltpu.VMEM((ra_rows, 1), F32), pltpu.VMEM((ra_rows, 1), F32), pltpu.VMEM((ra_rows, wa), F32),
               pltpu.VMEM((ra_rows, 1), F32),
               pltpu.VMEM((rc_rows, 1), F32), pltpu.VMEM((rc_rows, 1), F32), pltpu.VMEM((rc_rows, PAIR), F32)]
    return pl.pallas_call(
        functools.partial(_sample_kernel, pages_per_step=gsz, n_pages=n_pages, out_scale=out_scale),
        grid_spec=pltpu.PrefetchScalarGridSpec(
            num_scalar_prefetch=1,
            grid=(dbsz, steps),
            in_specs=in_specs,
            out_specs=[seq_spec(wa), seq_spec(wa), seq_spec(2 * wa)],
            scratch_shapes=scratch),
        out_shape=[jax.ShapeDtypeStruct((dbsz, nq, wa), F32), jax.ShapeDtypeStruct((dbsz, nq, wa), F32),
                   jax.ShapeDtypeStruct((dbsz, nq, 2 * wa), F32)],
        compiler_params=_params(("arbitrary", "arbitrary")),
        name="sample_mixers",
    )(page_table, lam, abf, bbf, cbf, csq, csk, ltri, gsub, *pools)


def _merge_kernel(h_ref, oa_ref, ob_ref, oc_ref, g1_ref, wa_ref, wb_ref, wc_ref, wg_ref, bg_ref, wo_ref,
                  g2_ref, wrh_ref, wrl_ref, br_ref, h2_ref, x2_ref, gates_ref):
    x = h_ref[...]
    d = x.shape[1]
    xb = _rms_rows(x, g1_ref[...]).astype(BF16)
    ys = (_dot(oa_ref[...].astype(BF16), wa_ref[...]),
          _dot(ob_ref[...].astype(BF16), wb_ref[...]),
          _dot(oc_ref[...].astype(BF16), wc_ref[...]))
    merged = jnp.zeros_like(x)
    for idx, y in enumerate(ys):
        gl = _dot(xb, wg_ref[:, idx * d:(idx + 1) * d]) + bg_ref[:, idx * d:(idx + 1) * d]
        merged = merged + y / (1.0 + jnp.exp(-gl))
    h2 = x + _dot(merged.astype(BF16), wo_ref[...])
    h2_ref[...] = h2
    x2 = _rms_rows(h2, g2_ref[...])
    x2_ref[...] = x2.astype(BF16)

    xh, xl = _split(x2)
    wrh = wrh_ref[...]
    logits = _dot(xh, wrh) + _dot(xl, wrh) + _dot(xh, wrl_ref[...]) + br_ref[...]
    lane = lax.broadcasted_iota(jnp.int32, (1, LANES), 1)
    lane_f = lane.astype(F32)
    n_exp = N_GROUPS * EXPERTS_PER_GROUP
    is_exp = jnp.logical_and(lane >= GATE_LANE0, lane < GATE_LANE0 + n_exp)
    grp_of_lane = jnp.where(is_exp, ((lane - GATE_LANE0) >> GROUP_SHIFT).astype(F32), -1.0)

    def first_max(vals):
        mx = jnp.max(vals, axis=1, keepdims=True)
        idx = jnp.min(jnp.where(vals == mx, lane_f, float(LANES)), axis=1, keepdims=True)
        return mx, idx

    glog = jnp.where(lane < N_GROUPS, logits, NEG)
    gmax, gidx = first_max(glog)
    p_g = 1.0 / jnp.sum(jnp.exp(glog - gmax), axis=1, keepdims=True)
    elog = jnp.where(grp_of_lane == gidx, logits, NEG)
    m1, i1 = first_max(elog)
    elog2 = jnp.where(lane_f == i1, NEG, elog)
    m2, i2 = first_max(elog2)
    e2 = jnp.exp(m2 - m1)
    w1 = p_g / (1.0 + e2)
    w2 = p_g * e2 / (1.0 + e2)
    gates_ref[...] = jnp.where(lane_f == i1, w1, 0.0) + jnp.where(lane_f == i2, w2, 0.0)


def _merge(h2d, oa, ob, oc, g1, wa, wb, wc, wg, bg, wo, g2, wrh, wrl, br, tm):
    n, d = h2d.shape
    const = lambda a: pl.BlockSpec(a.shape, lambda i: (0,) * a.ndim)
    row = lambda w: pl.BlockSpec((tm, w), lambda i: (i, 0))
    return pl.pallas_call(
        _merge_kernel,
        grid=(n // tm,),
        in_specs=[row(d), row(oa.shape[1]), row(ob.shape[1]), row(oc.shape[1]), const(g1), const(wa),
                  const(wb), const(wc), const(wg), const(bg), const(wo), const(g2), const(wrh),
                  const(wrl), const(br)],
        out_specs=[row(d), row(d), row(LANES)],
        out_shape=[jax.ShapeDtypeStruct((n, d), F32), jax.ShapeDtypeStruct((n, d), BF16),
                   jax.ShapeDtypeStruct((n, LANES), F32)],
        compiler_params=_params(("arbitrary",)),
        name="merge",
    )(h2d, oa, ob, oc, g1, wa, wb, wc, wg, bg, wo, g2, wrh, wrl, br)


def _moe_kernel(x2_ref, gates_ref, h2_ref, wg_ref, wu_ref, wd_ref, o_ref):
    g = pl.program_id(1)
    x = x2_ref[...]
    hg = _dot(x, wg_ref[...])
    hu = _dot(x, wu_ref[...])
    act = hg / (1.0 + jnp.exp(-hg)) * hu
    gates = gates_ref[...]
    lane = lax.broadcasted_iota(jnp.int32, (1, LANES), 1)
    parts = []
    for e in range(EXPERTS_PER_GROUP):
        col = jnp.sum(jnp.where(lane == GATE_LANE0 + g * EXPERTS_PER_GROUP + e, gates, 0.0),
                      axis=1, keepdims=True)
        parts.append((act[:, e * D_EXPERT:(e + 1) * D_EXPERT] * col).astype(BF16))
    y = _dot(jnp.concatenate(parts, axis=1), wd_ref[...])

    @pl.when(g == 0)
    def _():
        o_ref[...] = h2_ref[...] + y

    @pl.when(g != 0)
    def _():
        o_ref[...] = o_ref[...] + y


def _moe(x2, gates, h2, wg, wu, wd, tm):
    n, d = h2.shape
    gw = EXPERTS_PER_GROUP * D_EXPERT
    return pl.pallas_call(
        _moe_kernel,
        grid=(n // tm, N_GROUPS),
        in_specs=[pl.BlockSpec((tm, d), lambda i, g: (i, 0)), pl.BlockSpec((tm, LANES), lambda i, g: (i, 0)),
                  pl.BlockSpec((tm, d), lambda i, g: (i, 0)), pl.BlockSpec((d, gw), lambda i, g: (0, g)),
                  pl.BlockSpec((d, gw), lambda i, g: (0, g)), pl.BlockSpec((gw, d), lambda i, g: (g, 0))],
        out_specs=pl.BlockSpec((tm, d), lambda i, g: (i, 0)),
        out_shape=jax.ShapeDtypeStruct((n, d), F32),
        compiler_params=_params(("arbitrary", "arbitrary")),
        name="moe",
    )(x2, gates, h2, wg, wu, wd)


def _pad_lanes(x, lane0=0):
    return jnp.pad(x, ((0, 0), (lane0, LANES - lane0 - x.shape[1])))


def kernel(x_prompt, x_sample, cache_a_kv, cache_b_kv, cache_b_logf, cache_c_kv, page_table, norm1_g, w_in, b_forget, qn_b_g, kn_b_g, qn_c_g, kn_c_g, lam_q1, lam_k1, lam_q2, lam_k2, subln_c_g, w_br_a, w_br_b, w_br_c, w_gate, b_gate, w_out, norm2_g, w_router_grp, b_router_grp, w_router_exp, b_router_exp, w_exp_gate, w_exp_up, w_exp_down):
    bsz, seq, d = x_prompt.shape
    dbsz, nq, _ = x_sample.shape
    depth = w_in.shape[0]
    n_pool, page = cache_a_kv.shape[1], cache_a_kv.shape[2]
    wa = N_HEADS * HEAD_DIM
    n_exp = N_GROUPS * EXPERTS_PER_GROUP
    assert nq == 8 and d % LANES == 0

    tm_p = 512
    tq_a, tq = 128, 256
    ns = dbsz * nq
    pages_per_step = 4

    pool_a = jnp.transpose(cache_a_kv, (0, 1, 3, 4, 5, 2)).reshape(depth, n_pool, 2, wa, page)
    pool_b = jnp.transpose(cache_b_kv, (0, 1, 3, 4, 5, 2)).reshape(depth, n_pool, 2, wa, page)
    pool_f = jnp.transpose(cache_b_logf, (0, 1, 3, 2))
    pool_c = cache_c_kv.reshape(depth, n_pool, page * 2 * N_HEADS, PAIR)

    seg = (jnp.kron(jnp.eye(N_HEADS, dtype=F32), jnp.ones((HEAD_DIM, HEAD_DIM), F32)) / HEAD_DIM).astype(BF16)
    tri_p = jnp.tril(jnp.ones((tm_p, tm_p), F32)).astype(BF16)
    r = jnp.arange(ns)
    tri_s = jnp.logical_and(r[:, None] >= r[None, :], (r[:, None] // nq) == (r[None, :] // nq)).astype(BF16)
    ltri_a = jnp.tril(jnp.ones((tq_a, tq_a), F32), -1).astype(BF16)
    ltri_pg = jnp.tril(jnp.ones((page, page), F32), -1).astype(BF16)
    slopes = jnp.asarray(ALIBI_SLOPES, F32)

    hp = x_prompt.reshape(bsz * seq, d)
    hs = x_sample.reshape(ns, d)
    rows_p = [[], [], [], []]
    rows_s = [[], [], [], []]

    for l in range(depth):
        lam_init = 0.8 - 0.6 * math.exp(-0.3 * l)
        lam = (jnp.exp(jnp.sum(lam_q1[l].astype(F32) * lam_k1[l].astype(F32)))
               - jnp.exp(jnp.sum(lam_q2[l].astype(F32) * lam_k2[l].astype(F32))) + lam_init).reshape(1)
        out_scale = 1.0 - lam_init

        w = w_in[l]
        off_f = 6 * wa
        wm = jnp.concatenate([w[:, :off_f], w[:, off_f + N_HEADS:]], axis=1).astype(BF16)
        wf = _pad_lanes(w[:, off_f:off_f + N_HEADS]).astype(BF16)
        bfp = _pad_lanes(b_forget[l].reshape(1, N_HEADS).astype(F32))
        g1 = norm1_g[l].reshape(1, d)
        g2 = norm2_g[l].reshape(1, d)
        gb = jnp.stack([jnp.tile(qn_b_g[l], N_HEADS), jnp.tile(kn_b_g[l], N_HEADS)])
        gc = jnp.stack([jnp.tile(qn_c_g[l], 2 * N_HEADS), jnp.tile(kn_c_g[l], 2 * N_HEADS)])
        gsub = subln_c_g[l].reshape(1, PAIR)
        wr = jnp.concatenate([_pad_lanes(w_router_grp[l])[:, :GATE_LANE0],
                              _pad_lanes(w_router_exp[l], GATE_LANE0)[:, GATE_LANE0:]], axis=1)
        wrh = wr.astype(BF16)
        wrl = (wr - wrh.astype(F32)).astype(BF16)
        br = (_pad_lanes(b_router_grp[l].reshape(1, N_GROUPS))
              + _pad_lanes(b_router_exp[l].reshape(1, n_exp), GATE_LANE0))
        weg = jnp.transpose(w_exp_gate[l], (1, 0, 2)).reshape(d, n_exp * D_EXPERT).astype(BF16)
        weu = jnp.transpose(w_exp_up[l], (1, 0, 2)).reshape(d, n_exp * D_EXPERT).astype(BF16)
        wed = w_exp_down[l].reshape(n_exp * D_EXPERT, d).astype(BF16)
        merge_w = (g1, w_br_a[l].astype(BF16), w_br_b[l].astype(BF16), w_br_c[l].astype(BF16),
                   w_gate[l].astype(BF16), b_gate[l].reshape(1, -1), w_out[l].astype(BF16), g2, wrh, wrl, br)
        proj_w = (g1, wm, wf, bfp, gb, gc, seg)

        arow, brow, crow, lf, cs, abf, bbf, cbf = _proj(hp, tri_p, seq // tm_p, *proj_w)
        ck = jnp.transpose(cs.reshape(bsz, seq, LANES)[:, :, :8], (0, 2, 1))
        oa, ob, oc = _prompt_mixers(abf.reshape(bsz, seq, -1), bbf.reshape(bsz, seq, -1),
                                    cbf.reshape(bsz, seq, -1), cs.reshape(bsz, seq, LANES), ck,
                                    ltri_a, slopes, lam, gsub, out_scale, tq_a, tq)
        h2, x2, gates = _merge(hp, oa.reshape(bsz * seq, -1), ob.reshape(bsz * seq, -1),
                               oc.reshape(bsz * seq, -1), *merge_w, tm_p)
        hp = _moe(x2, gates, h2, weg, weu, wed, tm_p)
        rows_p[0].append(arow.reshape(bsz, seq, 2, N_HEADS, HEAD_DIM))
        rows_p[1].append(brow.reshape(bsz, seq, 2, N_HEADS, HEAD_DIM))
        rows_p[2].append(lf[:, :N_HEADS].reshape(bsz, seq, N_HEADS))
        rows_p[3].append(crow.reshape(bsz, seq, 2, N_HEADS, 2 * HEAD_DIM))

        arow, brow, crow, lf, cs, abf, bbf, cbf = _proj(hs, tri_s, 1, *proj_w)
        csq = cs.reshape(dbsz, nq, LANES)
        csk = jnp.pad(jnp.transpose(csq[:, :, :8], (0, 2, 1)), ((0, 0), (0, 0), (0, LANES - nq)))
        oa, ob, oc = _sample_mixers(l, page_table, lam, abf.astype(F32).reshape(dbsz, nq, -1),
                                    bbf.astype(F32).reshape(dbsz, nq, -1),
                                    cbf.astype(F32).reshape(dbsz, nq, -1), csq, csk, ltri_pg, gsub,
                                    pool_a, pool_b, pool_f, pool_c, out_scale, pages_per_step)
        h2, x2, gates = _merge(hs, oa.reshape(ns, -1), ob.reshape(ns, -1), oc.reshape(ns, -1), *merge_w, ns)
        hs = _moe(x2, gates, h2, weg, weu, wed, ns)
        rows_s[0].append(arow.reshape(dbsz, nq, 2, N_HEADS, HEAD_DIM))
        rows_s[1].append(brow.reshape(dbsz, nq, 2, N_HEADS, HEAD_DIM))
        rows_s[2].append(lf[:, :N_HEADS].reshape(dbsz, nq, N_HEADS))
        rows_s[3].append(crow.reshape(dbsz, nq, 2, N_HEADS, 2 * HEAD_DIM))

    return (hp.reshape(bsz, seq, d), hs.reshape(dbsz, nq, d),
            jnp.stack(rows_p[0]), jnp.stack(rows_p[1]), jnp.stack(rows_p[2]), jnp.stack(rows_p[3]),
            jnp.stack(rows_s[0]), jnp.stack(rows_s[1]), jnp.stack(rows_s[2]), jnp.stack(rows_s[3]))
```

```python
import functools
import math

import jax
import jax.numpy as jnp
from jax import lax
from jax.experimental import pallas as pl
from jax.experimental.pallas import tpu as pltpu

F32 = jnp.float32
BF16 = jnp.bfloat16

HEAD_DIM = 64
HEAD_SHIFT = 6
N_HEADS = 4
PAIR = 2 * HEAD_DIM
N_GROUPS = 4
EXPERTS_PER_GROUP = 8
GROUP_SHIFT = 3
D_EXPERT = 128
NORM_EPS = 1e-6
NEG = -1e30
LANES = 128
GATE_LANE0 = 8
STICK_SKIP = 110.0
ALIBI_SLOPES = tuple(2.0 ** (-8.0 * (h + 1) / N_HEADS) for h in range(N_HEADS))
VMEM_LIMIT = 56 * 1024 * 1024

TOKEN_TILE = 512
STICK_TILE = 256
SOFTMAX_TILE = 512
PAGES_PER_STEP = 8


def _dot(a, b):
    return jnp.dot(a, b, preferred_element_type=F32)


def _dot_nt(a, b):
    return lax.dot_general(a, b, (((1,), (1,)), ((), ())), preferred_element_type=F32)


def _split(x):
    hi = x.astype(BF16)
    lo = (x - hi.astype(F32)).astype(BF16)
    return hi, lo


def _log_sigmoid(x):
    return jnp.minimum(x, 0.0) - jnp.log1p(jnp.exp(-jnp.abs(x)))


def _rms_rows(x, g):
    return x * lax.rsqrt(jnp.mean(x * x, axis=-1, keepdims=True) + NORM_EPS) * g


def _params(sem):
    return pltpu.CompilerParams(dimension_semantics=sem, vmem_limit_bytes=VMEM_LIMIT)


def _iota(shape, dim):
    return lax.broadcasted_iota(jnp.int32, shape, dim)


def _proj_kernel(h_ref, g1_ref, wm_ref, wf_ref, bf_ref, gb_ref, gc_ref, seg_ref, tri_ref,
                 arow_ref, brow_ref, crow_ref, lf_ref, cs_ref, abf_ref, bbf_ref, cbf_ref,
                 carry_ref, *, tiles_per_seq):
    i = pl.program_id(0)
    tm = h_ref.shape[0]
    xb = _rms_rows(h_ref[...], g1_ref[...]).astype(BF16)
    seg = seg_ref[...]
    wa = N_HEADS * HEAD_DIM
    wc = 2 * wa

    def headnorm(u, g):
        hi, lo = _split(u * u)
        ms = _dot(hi, seg) + _dot(lo, seg)
        return u * lax.rsqrt(ms + NORM_EPS) * g

    ua = _dot(xb, wm_ref[:, 0:3 * wa])
    arow_ref[...] = ua[:, wa:3 * wa]
    abf_ref[:, 0:wa] = (ua[:, 0:wa] * 0.125).astype(BF16)
    abf_ref[:, wa:3 * wa] = ua[:, wa:3 * wa].astype(BF16)

    ub = _dot(xb, wm_ref[:, 3 * wa:6 * wa])
    qb = headnorm(ub[:, 0:wa], gb_ref[0:1, :])
    kb = headnorm(ub[:, wa:2 * wa], gb_ref[1:2, :])
    vb = ub[:, 2 * wa:3 * wa]
    brow_ref[:, 0:wa] = kb
    brow_ref[:, wa:2 * wa] = vb
    bbf_ref[:, 0:wa] = (qb * 0.125).astype(BF16)
    bbf_ref[:, wa:2 * wa] = kb.astype(BF16)
    bbf_ref[:, 2 * wa:3 * wa] = vb.astype(BF16)

    c0 = 6 * wa
    uc = _dot(xb, wm_ref[:, c0:c0 + 3 * wc])
    for half in range(2):
        sl = slice(half * wa, (half + 1) * wa)
        qc = headnorm(uc[:, half * wa:(half + 1) * wa], gc_ref[0:1, sl])
        kc = headnorm(uc[:, wc + half * wa:wc + (half + 1) * wa], gc_ref[1:2, sl])
        cbf_ref[:, half * wa:(half + 1) * wa] = (qc * 0.125).astype(BF16)
        cbf_ref[:, wc + half * wa:wc + (half + 1) * wa] = kc.astype(BF16)
        crow_ref[:, half * wa:(half + 1) * wa] = kc
    vc = uc[:, 2 * wc:3 * wc]
    crow_ref[:, wc:2 * wc] = vc
    cbf_ref[:, 2 * wc:3 * wc] = vc.astype(BF16)

    lf = _log_sigmoid(_dot(xb, wf_ref[...]) + bf_ref[...])
    lf_ref[...] = lf

    @pl.when(i % tiles_per_seq == 0)
    def _():
        carry_ref[...] = jnp.zeros_like(carry_ref)

    hi, lo = _split(lf)
    tri = tri_ref[...]
    cs = _dot(tri, hi) + _dot(tri, lo) + carry_ref[...]
    cs_ref[...] = cs
    carry_ref[...] = cs[tm - 1:tm, :]


def _proj(h2d, tri, tiles_per_seq, g1, wm, wf, bfp, gb, gc, seg):
    n, d = h2d.shape
    tm = tri.shape[0]
    wa = N_HEADS * HEAD_DIM
    const = lambda shape: pl.BlockSpec(shape, lambda i: (0,) * len(shape))
    row = lambda w: pl.BlockSpec((tm, w), lambda i: (i, 0))
    out_shapes = [
        jax.ShapeDtypeStruct((n, 2 * wa), F32), jax.ShapeDtypeStruct((n, 2 * wa), F32),
        jax.ShapeDtypeStruct((n, 4 * wa), F32), jax.ShapeDtypeStruct((n, LANES), F32),
        jax.ShapeDtypeStruct((n, LANES), F32), jax.ShapeDtypeStruct((n, 3 * wa), BF16),
        jax.ShapeDtypeStruct((n, 3 * wa), BF16), jax.ShapeDtypeStruct((n, 6 * wa), BF16),
    ]
    return pl.pallas_call(
        functools.partial(_proj_kernel, tiles_per_seq=tiles_per_seq),
        grid=(n // tm,),
        in_specs=[row(d), const((1, d)), const(wm.shape), const(wf.shape), const((1, LANES)),
                  const(gb.shape), const(gc.shape), const(seg.shape), const(tri.shape)],
        out_specs=[row(2 * wa), row(2 * wa), row(4 * wa), row(LANES), row(LANES),
                   row(3 * wa), row(3 * wa), row(6 * wa)],
        out_shape=out_shapes,
        scratch_shapes=[pltpu.VMEM((1, LANES), F32)],
        compiler_params=_params(("arbitrary",)),
        name="proj",
    )(h2d, g1, wm, wf, bfp, gb, gc, seg, tri)


def _head_mask(hh):
    return (_iota((1, PAIR), 1) >> HEAD_SHIFT) == hh


def _store_head(o_ref, hh, hmask, res):
    @pl.when(hh == 0)
    def _():
        o_ref[...] = jnp.where(hmask, res, 0.0).astype(o_ref.dtype)

    @pl.when(hh != 0)
    def _():
        o_ref[...] = jnp.where(hmask, res.astype(o_ref.dtype), o_ref[...])


def _softmax_init(rows, width=PAIR):
    return (jnp.full((rows, 1), NEG, F32), jnp.zeros((rows, 1), F32), jnp.zeros((rows, width), F32))


def _softmax_probs(m, l, s):
    m_new = jnp.maximum(m, jnp.max(s, axis=1, keepdims=True))
    alpha = jnp.exp(m - m_new)
    p = jnp.exp(s - m_new)
    return m_new, alpha * l + jnp.sum(p, axis=1, keepdims=True), alpha, p.astype(BF16)


def _softmax_step(state, s, v):
    m, l, acc = state
    m, l, alpha, p = _softmax_probs(m, l, s)
    return m, l, alpha * acc + _dot(p, v)


def _stick_logs(z, valid=None):
    lb = _log_sigmoid(z)
    ls = lb - z
    if valid is not None:
        ls = jnp.where(valid, ls, 0.0)
    return lb, ls


def _stick_weights(lb, ls, later, run, valid=None):
    w = jnp.exp(lb + later + run)
    if valid is not None:
        w = jnp.where(valid, w, 0.0)
    return w, run + jnp.sum(ls, axis=1, keepdims=True)


def _stick_kernel(q_ref, k_ref, v_ref, ltri_ref, o_ref):
    i = pl.program_id(2)
    hh = pl.program_id(3)
    tq = q_ref.shape[0]
    hmask = _head_mask(hh)
    q = q_ref[...]
    qm = jnp.where(hmask, q, jnp.zeros_like(q))
    ltri = ltri_ref[...]
    valid = _iota((tq, tq), 1) < _iota((tq, tq), 0)

    def block(j, run, acc, mask):
        off = pl.multiple_of(j * tq, tq)
        lb, ls = _stick_logs(_dot_nt(qm, k_ref[pl.ds(off, tq), :]), mask)
        hi, lo = _split(ls)
        w, run = _stick_weights(lb, ls, _dot(hi, ltri) + _dot(lo, ltri), run, mask)
        return run, acc + _dot(w.astype(BF16), v_ref[pl.ds(off, tq), :])

    run, acc = block(i, jnp.zeros((tq, 1), F32), jnp.zeros((tq, PAIR), F32), valid)

    def cond(c):
        j, run, _ = c
        return jnp.logical_and(j >= 0, jnp.max(run) > -STICK_SKIP)

    def body(c):
        j, run, acc = c
        run, acc = block(j, run, acc, None)
        return j - 1, run, acc

    _, _, acc = lax.while_loop(cond, body, (i - 1, run, acc))
    _store_head(o_ref, hh, hmask, acc)


def _fox_kernel(q_ref, k_ref, v_ref, cq_ref, ck_ref, o_ref):
    pair = pl.program_id(1)
    i = pl.program_id(2)
    hh = pl.program_id(3)
    h = pair * 2 + hh
    tq = q_ref.shape[0]
    hmask = _head_mask(hh)
    q = q_ref[...]
    qm = jnp.where(hmask, q, jnp.zeros_like(q))
    cq = jnp.sum(jnp.where(_iota((1, LANES), 1) == h, cq_ref[...], 0.0), axis=1, keepdims=True)
    causal = _iota((tq, tq), 1) <= _iota((tq, tq), 0)

    def scores(j, diag):
        off = pl.multiple_of(j * tq, tq)
        ck = ck_ref[pl.ds(h, 1), pl.ds(off, tq)]
        s = _dot_nt(qm, k_ref[pl.ds(off, tq), :]) + (cq - ck)
        return jnp.where(causal, s, NEG) if diag else s

    def update(state, s, j):
        return _softmax_step(state, s, v_ref[pl.ds(pl.multiple_of(j * tq, tq), tq), :])

    def body(t, c):
        s, state = c
        j = i - 1 - t
        return scores(j, False), update(state, s, j + 1)

    s, state = lax.fori_loop(0, i, body, (scores(i, True), _softmax_init(tq)))
    _, l, acc = update(state, s, 0)
    _store_head(o_ref, hh, hmask, acc / l)


def _diff_kernel(slope_ref, lam_ref, q_ref, k_ref, v_ref, g_ref, o_ref, *, out_scale):
    h = pl.program_id(1)
    i = pl.program_id(2)
    tq = q_ref.shape[0]
    slope = slope_ref[h]
    lam = lam_ref[0]
    q = q_ref[...]
    qms = [jnp.where(_head_mask(c), q, jnp.zeros_like(q)) for c in range(2)]
    row = _iota((tq, tq), 0)
    col = _iota((tq, tq), 1)
    causal = col <= row
    sd0 = slope * (row - col).astype(F32)

    def scores(j, diag):
        k = k_ref[pl.ds(pl.multiple_of(j * tq, tq), tq), :]
        sb = sd0 + slope * ((i - j) * tq).astype(F32)
        out = []
        for c in range(2):
            s = _dot_nt(qms[c], k) - sb
            out.append(jnp.where(causal, s, NEG) if diag else s)
        return tuple(out)

    def update(states, ss, j):
        v = v_ref[pl.ds(pl.multiple_of(j * tq, tq), tq), :]
        return tuple(_softmax_step(states[c], ss[c], v) for c in range(2))

    def body(t, c):
        ss, states = c
        j = i - 1 - t
        return scores(j, False), update(states, ss, j + 1)

    ss, states = lax.fori_loop(0, i, body, (scores(i, True), (_softmax_init(tq), _softmax_init(tq))))
    (_, l0, a0), (_, l1, a1) = update(states, ss, 0)
    o = a0 / l0 - lam * (a1 / l1)
    o_ref[...] = (_rms_rows(o, g_ref[...]) * out_scale).astype(o_ref.dtype)


def _prompt_mixers(abf, bbf, cbf, cs, ck, ltri, slopes, lam, gsub, out_scale):
    bsz, t, _ = abf.shape
    npair = N_HEADS // 2
    wa = N_HEADS * HEAD_DIM

    def pair_specs(tile):
        qs = pl.BlockSpec((None, tile, PAIR), lambda b, p, i, hh: (b, i, p))
        ks = pl.BlockSpec((None, t, PAIR), lambda b, p, i, hh: (b, 0, npair + p))
        vs = pl.BlockSpec((None, t, PAIR), lambda b, p, i, hh: (b, 0, 2 * npair + p))
        os = pl.BlockSpec((None, tile, PAIR), lambda b, p, i, hh: (b, i, p))
        return qs, ks, vs, os

    sem4 = ("arbitrary",) * 4
    qs, ks, vs, os = pair_specs(STICK_TILE)
    oa = pl.pallas_call(
        _stick_kernel,
        grid=(bsz, npair, t // STICK_TILE, 2),
        in_specs=[qs, ks, vs, pl.BlockSpec(ltri.shape, lambda b, p, i, hh: (0, 0))],
        out_specs=os,
        out_shape=jax.ShapeDtypeStruct((bsz, t, wa), BF16),
        compiler_params=_params(sem4),
        name="stick_prompt",
    )(abf, abf, abf, ltri)

    tq = SOFTMAX_TILE
    qs, ks, vs, os = pair_specs(tq)
    ob = pl.pallas_call(
        _fox_kernel,
        grid=(bsz, npair, t // tq, 2),
        in_specs=[qs, ks, vs,
                  pl.BlockSpec((None, tq, LANES), lambda b, p, i, hh: (b, i, 0)),
                  pl.BlockSpec((None, 8, t), lambda b, p, i, hh: (b, 0, 0))],
        out_specs=os,
        out_shape=jax.ShapeDtypeStruct((bsz, t, wa), BF16),
        compiler_params=_params(sem4),
        name="fox_prompt",
    )(bbf, bbf, bbf, cs, ck)

    smem = pl.BlockSpec(memory_space=pltpu.SMEM)
    oc = pl.pallas_call(
        functools.partial(_diff_kernel, out_scale=out_scale),
        grid=(bsz, N_HEADS, t // tq),
        in_specs=[smem, smem,
                  pl.BlockSpec((None, tq, PAIR), lambda b, h, i: (b, i, h)),
                  pl.BlockSpec((None, t, PAIR), lambda b, h, i: (b, 0, N_HEADS + h)),
                  pl.BlockSpec((None, t, PAIR), lambda b, h, i: (b, 0, 2 * N_HEADS + h)),
                  pl.BlockSpec((1, PAIR), lambda b, h, i: (0, 0))],
        out_specs=pl.BlockSpec((None, tq, PAIR), lambda b, h, i: (b, i, h)),
        out_shape=jax.ShapeDtypeStruct((bsz, t, 2 * wa), BF16),
        compiler_params=_params(("arbitrary",) * 3),
        name="diff_prompt",
    )(slopes, lam, cbf, cbf, cbf, gsub)
    return oa, ob, oc


def _sample_kernel(pt_ref, lam_ref, qa_ref, qb_ref, qc_ref, csq_ref, csk_ref, ltri_ref, g_ref, *rest,
                   pages_per_step, n_pages, out_scale):
    del pt_ref
    gsz = pages_per_step
    page_refs = rest[:4 * gsz]
    oa_ref, ob_ref, oc_ref = rest[4 * gsz:4 * gsz + 3]
    (ra_ref, acca_ref, mb_ref, lb_ref, accb_ref, carry_ref,
     mc_ref, lc_ref, accc_ref) = rest[4 * gsz + 3:]
    j = pl.program_id(1)
    nq = qa_ref.shape[0]
    span = ltri_ref.shape[0]
    page = span // gsz
    wa = N_HEADS * HEAD_DIM
    wc = 2 * wa
    ra_rows = N_HEADS * nq
    rc_rows = 2 * ra_rows
    nq_shift = nq.bit_length() - 1
    lam = lam_ref[0]

    def stacked_q(x, reps):
        return jnp.concatenate([x] * reps, axis=0)

    def pad_rows(x):
        return jnp.concatenate([x, jnp.zeros((page - nq, x.shape[1]), x.dtype)], axis=0)

    def expand_heads(x):
        return jnp.concatenate(
            [jnp.broadcast_to(x[h:h + 1, :], (nq, x.shape[1])) for h in range(N_HEADS)], axis=0)

    qa = qa_ref[...]
    qb = qb_ref[...]
    qc = qc_ref[...]
    diag_a = (_iota((ra_rows, wa), 1) >> HEAD_SHIFT) == (_iota((ra_rows, wa), 0) >> nq_shift)
    diag_c = (_iota((rc_rows, wc), 1) >> HEAD_SHIFT) == (_iota((rc_rows, wc), 0) >> nq_shift)
    qma = jnp.where(diag_a, stacked_q(qa[:, 0:wa], N_HEADS), 0.0).astype(BF16)
    qmb = jnp.where(diag_a, stacked_q(qb[:, 0:wa], N_HEADS), 0.0).astype(BF16)
    qmc = jnp.where(diag_c, stacked_q(qc[:, 0:wc], 2 * N_HEADS), 0.0).astype(BF16)
    head_c = _iota((rc_rows, 1), 0) >> (nq_shift + 1)
    slope_c = jnp.full((rc_rows, 1), ALIBI_SLOPES[N_HEADS - 1], F32)
    for h in range(N_HEADS - 1):
        slope_c = jnp.where(head_c == h, ALIBI_SLOPES[h], slope_c)

    csq = csq_ref[...]
    lane8 = _iota((nq, LANES), 1)
    fnew = jnp.concatenate(
        [jnp.sum(jnp.where(lane8 == h, csq, 0.0), axis=1, keepdims=True) for h in range(N_HEADS)],
        axis=0)

    @pl.when(j == 0)
    def _():
        lane_k = _iota((ra_rows, page), 1)
        qi_a = _iota((ra_rows, page), 0) & (nq - 1)
        ltri = ltri_ref[0:page, 0:page]
        ka = pad_rows(qa[:, wa:2 * wa]).astype(BF16)
        va = pad_rows(qa[:, 2 * wa:3 * wa]).astype(BF16)
        lb, ls = _stick_logs(_dot_nt(qma, ka), lane_k < qi_a)
        hi, lo = _split(ls)
        w, run = _stick_weights(lb, ls, _dot(hi, ltri) + _dot(lo, ltri), jnp.zeros((ra_rows, 1), F32),
                                lane_k < qi_a)
        ra_ref[...] = run
        acca_ref[...] = _dot(w.astype(BF16), va)

        kb = pad_rows(qb[:, wa:2 * wa]).astype(BF16)
        vb = pad_rows(qb[:, 2 * wa:3 * wa]).astype(BF16)
        s = _dot_nt(qmb, kb) + (fnew - expand_heads(csk_ref[...]))
        m, l, acc = _softmax_step(_softmax_init(ra_rows, wa), jnp.where(lane_k <= qi_a, s, NEG), vb)
        mb_ref[...] = m
        lb_ref[...] = l
        accb_ref[...] = acc
        carry_ref[...] = jnp.zeros_like(carry_ref)

        lane_kc = _iota((rc_rows, page), 1)
        qi_c = _iota((rc_rows, page), 0) & (nq - 1)
        kc = pad_rows(qc[:, wc:2 * wc]).astype(BF16)
        vc = pad_rows(qc[:, 2 * wc:3 * wc]).astype(BF16)
        s = _dot_nt(qmc, kc) - slope_c * (qi_c - lane_kc).astype(F32)
        m, l, acc = _softmax_step(_softmax_init(rc_rows, wc), jnp.where(lane_kc <= qi_c, s, NEG), vc)
        mc_ref[...] = m
        lc_ref[...] = l
        accc_ref[...] = acc

    a_refs = page_refs[0::4]
    b_refs = page_refs[1::4]
    f_refs = page_refs[2::4]
    c_refs = page_refs[3::4]
    first_page = n_pages - (j + 1) * gsz
    ltri = ltri_ref[...]

    def lanes_cat(refs, idx):
        return jnp.concatenate([r[idx].astype(BF16) for r in refs], axis=1)

    def rows_cat(kv):
        return jnp.concatenate(
            [jnp.concatenate([r[pl.ds(kv * N_HEADS + h, page, stride=2 * N_HEADS), :].astype(BF16)
                              for h in range(N_HEADS)], axis=1) for r in c_refs], axis=0)

    z_a = _dot(qma, lanes_cat(a_refs, 0))
    s_b = _dot(qmb, lanes_cat(b_refs, 0))
    s_c = _dot_nt(qmc, rows_cat(0))

    lb, ls = _stick_logs(z_a)
    lf_r = expand_heads(jnp.concatenate([r[...] for r in f_refs], axis=1))
    cum = _dot(jnp.concatenate(_split(ls) + _split(lf_r), axis=0), ltri)
    later = cum[0:ra_rows] + cum[ra_rows:2 * ra_rows]
    sfx_r = cum[2 * ra_rows:3 * ra_rows] + cum[3 * ra_rows:4 * ra_rows]
    w, run = _stick_weights(lb, ls, later, ra_ref[...])
    ra_ref[...] = run

    carry = carry_ref[...]
    m_b, l_b, alpha_b, p_b = _softmax_probs(mb_ref[...], lb_ref[...], s_b + (sfx_r + (fnew + carry)))
    mb_ref[...] = m_b
    lb_ref[...] = l_b
    carry_ref[...] = carry + (sfx_r[:, 0:1] + lf_r[:, 0:1])

    qi_c = _iota((rc_rows, span), 0) & (nq - 1)
    dist = (qi_c - _iota((rc_rows, span), 1) + (n_pages - first_page) * page).astype(F32)
    m_c, l_c, alpha_c, p_c = _softmax_probs(mc_ref[...], lc_ref[...], s_c - slope_c * dist)
    mc_ref[...] = m_c
    lc_ref[...] = l_c

    acca_ref[...] = acca_ref[...] + _dot_nt(w.astype(BF16), lanes_cat(a_refs, 1))
    accb_ref[...] = alpha_b * accb_ref[...] + _dot_nt(p_b, lanes_cat(b_refs, 1))
    accc_ref[...] = alpha_c * accc_ref[...] + _dot(p_c, rows_cat(1))

    @pl.when(j == pl.num_programs(1) - 1)
    def _():
        lane_o = _iota((nq, wa), 1) >> HEAD_SHIFT
        acc_a = acca_ref[...]
        nb = accb_ref[...] / lb_ref[...]
        oa = jnp.zeros((nq, wa), F32)
        ob = jnp.zeros((nq, wa), F32)
        for h in range(N_HEADS):
            oa = jnp.where(lane_o == h, acc_a[h * nq:(h + 1) * nq, :], oa)
            ob = jnp.where(lane_o == h, nb[h * nq:(h + 1) * nq, :], ob)
        oa_ref[...] = oa
        ob_ref[...] = ob
        nc = accc_ref[...] / lc_ref[...]
        for h in range(N_HEADS):
            blk = nc[2 * h * nq:2 * (h + 1) * nq, h * PAIR:(h + 1) * PAIR]
            o = blk[0:nq, :] - lam * blk[nq:2 * nq, :]
            oc_ref[:, h * PAIR:(h + 1) * PAIR] = _rms_rows(o, g_ref[...]) * out_scale


def _sample_mixers(layer, page_table, lam, abf, bbf, cbf, csq, csk, ltri, gsub,
                   pool_a, pool_b, pool_f, pool_c, out_scale):
    dbsz, nq, _ = abf.shape
    n_pages = page_table.shape[1]
    gsz = PAGES_PER_STEP
    page = ltri.shape[0] // gsz
    wa = N_HEADS * HEAD_DIM
    steps = n_pages // gsz

    def seq_spec(w):
        return pl.BlockSpec((None, nq, w), lambda b, j, pt: (b, 0, 0))

    def page_specs(g):
        def idx(b, j, pt):
            return pt[b, n_pages - (j + 1) * gsz + g]
        return [
            pl.BlockSpec((None, None, 2, wa, page), lambda b, j, pt: (layer, idx(b, j, pt), 0, 0, 0)),
            pl.BlockSpec((None, None, 2, wa, page), lambda b, j, pt: (layer, idx(b, j, pt), 0, 0, 0)),
            pl.BlockSpec((None, None, N_HEADS, page), lambda b, j, pt: (layer, idx(b, j, pt), 0, 0)),
            pl.BlockSpec((None, None, 2 * N_HEADS * page, PAIR),
                         lambda b, j, pt: (layer, idx(b, j, pt), 0, 0)),
        ]

    in_specs = [pl.BlockSpec(memory_space=pltpu.SMEM), seq_spec(3 * wa), seq_spec(3 * wa), seq_spec(6 * wa),
                seq_spec(LANES), pl.BlockSpec((None, 8, LANES), lambda b, j, pt: (b, 0, 0)),
                pl.BlockSpec(ltri.shape, lambda b, j, pt: (0, 0)),
                pl.BlockSpec((1, PAIR), lambda b, j, pt: (0, 0))]
    pools = []
    for g in range(gsz):
        in_specs += page_specs(g)
        pools += [pool_a, pool_b, pool_f, pool_c]
    ra_rows = N_HEADS * nq
    rc_rows = 2 * ra_rows
    scratch = [pltpu.VMEM((ra_rows, 1), F32), pltpu.VMEM((ra_rows, wa), F32),
               pltpu.VMEM((ra_rows, 1), F32), pltpu.VMEM((ra_rows, 1), F32), pltpu.VMEM((ra_rows, wa), F32),
               pltpu.VMEM((ra_rows, 1), F32),
               pltpu.VMEM((rc_rows, 1), F32), pltpu.VMEM((rc_rows, 1), F32), pltpu.VMEM((rc_rows, 2 * wa), F32)]
    return pl.pallas_call(
        functools.partial(_sample_kernel, pages_per_step=gsz, n_pages=n_pages, out_scale=out_scale),
        grid_spec=pltpu.PrefetchScalarGridSpec(
            num_scalar_prefetch=1,
            grid=(dbsz, steps),
            in_specs=in_specs,
            out_specs=[seq_spec(wa), seq_spec(wa), seq_spec(2 * wa)],
            scratch_shapes=scratch),
        out_shape=[jax.ShapeDtypeStruct((dbsz, nq, wa), F32), jax.ShapeDtypeStruct((dbsz, nq, wa), F32),
                   jax.ShapeDtypeStruct((dbsz, nq, 2 * wa), F32)],
        compiler_params=_params(("arbitrary", "arbitrary")),
        name="sample_mixers",
    )(page_table, lam, abf, bbf, cbf, csq, csk, ltri, gsub, *pools)


def _merge_kernel(h_ref, oa_ref, ob_ref, oc_ref, g1_ref, wa_ref, wb_ref, wc_ref, wg_ref, bg_ref, wo_ref,
                  g2_ref, wrh_ref, wrl_ref, br_ref, h2_ref, x2_ref, gates_ref):
    x = h_ref[...]
    d = x.shape[1]
    xb = _rms_rows(x, g1_ref[...]).astype(BF16)
    ys = (_dot(oa_ref[...].astype(BF16), wa_ref[...]),
          _dot(ob_ref[...].astype(BF16), wb_ref[...]),
          _dot(oc_ref[...].astype(BF16), wc_ref[...]))
    merged = jnp.zeros_like(x)
    for idx, y in enumerate(ys):
        gl = _dot(xb, wg_ref[:, idx * d:(idx + 1) * d]) + bg_ref[:, idx * d:(idx + 1) * d]
        merged = merged + y / (1.0 + jnp.exp(-gl))
    h2 = x + _dot(merged.astype(BF16), wo_ref[...])
    h2_ref[...] = h2
    x2 = _rms_rows(h2, g2_ref[...])
    x2_ref[...] = x2.astype(BF16)

    xh, xl = _split(x2)
    wrh = wrh_ref[...]
    logits = _dot(xh, wrh) + _dot(xl, wrh) + _dot(xh, wrl_ref[...]) + br_ref[...]
    lane = _iota((1, LANES), 1)
    lane_f = lane.astype(F32)
    n_exp = N_GROUPS * EXPERTS_PER_GROUP
    is_exp = jnp.logical_and(lane >= GATE_LANE0, lane < GATE_LANE0 + n_exp)
    grp_of_lane = jnp.where(is_exp, ((lane - GATE_LANE0) >> GROUP_SHIFT).astype(F32), -1.0)

    def first_max(vals):
        mx = jnp.max(vals, axis=1, keepdims=True)
        idx = jnp.min(jnp.where(vals == mx, lane_f, float(LANES)), axis=1, keepdims=True)
        return mx, idx

    glog = jnp.where(lane < N_GROUPS, logits, NEG)
    gmax, gidx = first_max(glog)
    p_g = 1.0 / jnp.sum(jnp.exp(glog - gmax), axis=1, keepdims=True)
    elog = jnp.where(grp_of_lane == gidx, logits, NEG)
    m1, i1 = first_max(elog)
    elog2 = jnp.where(lane_f == i1, NEG, elog)
    m2, i2 = first_max(elog2)
    e2 = jnp.exp(m2 - m1)
    w1 = p_g / (1.0 + e2)
    w2 = p_g * e2 / (1.0 + e2)
    gates_ref[...] = jnp.where(lane_f == i1, w1, 0.0) + jnp.where(lane_f == i2, w2, 0.0)


def _merge(h2d, oa, ob, oc, g1, wa, wb, wc, wg, bg, wo, g2, wrh, wrl, br, tm):
    n, d = h2d.shape
    const = lambda a: pl.BlockSpec(a.shape, lambda i: (0,) * a.ndim)
    row = lambda w: pl.BlockSpec((tm, w), lambda i: (i, 0))
    return pl.pallas_call(
        _merge_kernel,
        grid=(n // tm,),
        in_specs=[row(d), row(oa.shape[1]), row(ob.shape[1]), row(oc.shape[1]), const(g1), const(wa),
                  const(wb), const(wc), const(wg), const(bg), const(wo), const(g2), const(wrh),
                  const(wrl), const(br)],
        out_specs=[row(d), row(d), row(LANES)],
        out_shape=[jax.ShapeDtypeStruct((n, d), F32), jax.ShapeDtypeStruct((n, d), BF16),
                   jax.ShapeDtypeStruct((n, LANES), F32)],
        compiler_params=_params(("arbitrary",)),
        name="merge",
    )(h2d, oa, ob, oc, g1, wa, wb, wc, wg, bg, wo, g2, wrh, wrl, br)


def _moe_kernel(x2_ref, gates_ref, h2_ref, wg_ref, wu_ref, wd_ref, o_ref):
    g = pl.program_id(1)
    x = x2_ref[...]
    hg = _dot(x, wg_ref[...])
    hu = _dot(x, wu_ref[...])
    act = hg / (1.0 + jnp.exp(-hg)) * hu
    gates = gates_ref[...]
    lane = _iota((1, LANES), 1)
    parts = []
    for e in range(EXPERTS_PER_GROUP):
        col = jnp.sum(jnp.where(lane == GATE_LANE0 + g * EXPERTS_PER_GROUP + e, gates, 0.0),
                      axis=1, keepdims=True)
        parts.append((act[:, e * D_EXPERT:(e + 1) * D_EXPERT] * col).astype(BF16))
    y = _dot(jnp.concatenate(parts, axis=1), wd_ref[...])

    @pl.when(g == 0)
    def _():
        o_ref[...] = h2_ref[...] + y

    @pl.when(g != 0)
    def _():
        o_ref[...] = o_ref[...] + y


def _moe(x2, gates, h2, wg, wu, wd, tm):
    n, d = h2.shape
    gw = EXPERTS_PER_GROUP * D_EXPERT
    return pl.pallas_call(
        _moe_kernel,
        grid=(n // tm, N_GROUPS),
        in_specs=[pl.BlockSpec((tm, d), lambda i, g: (i, 0)), pl.BlockSpec((tm, LANES), lambda i, g: (i, 0)),
                  pl.BlockSpec((tm, d), lambda i, g: (i, 0)), pl.BlockSpec((d, gw), lambda i, g: (0, g)),
                  pl.BlockSpec((d, gw), lambda i, g: (0, g)), pl.BlockSpec((gw, d), lambda i, g: (g, 0))],
        out_specs=pl.BlockSpec((tm, d), lambda i, g: (i, 0)),
        out_shape=jax.ShapeDtypeStruct((n, d), F32),
        compiler_params=_params(("arbitrary", "arbitrary")),
        name="moe",
    )(x2, gates, h2, wg, wu, wd)


def _pad_lanes(x, lane0=0):
    return jnp.pad(x, ((0, 0), (lane0, LANES - lane0 - x.shape[1])))


def _strict_lower(n):
    return jnp.tril(jnp.ones((n, n), F32), -1).astype(BF16)


def kernel(x_prompt, x_sample, cache_a_kv, cache_b_kv, cache_b_logf, cache_c_kv, page_table, norm1_g, w_in, b_forget, qn_b_g, kn_b_g, qn_c_g, kn_c_g, lam_q1, lam_k1, lam_q2, lam_k2, subln_c_g, w_br_a, w_br_b, w_br_c, w_gate, b_gate, w_out, norm2_g, w_router_grp, b_router_grp, w_router_exp, b_router_exp, w_exp_gate, w_exp_up, w_exp_down):
    bsz, seq, d = x_prompt.shape
    dbsz, nq, _ = x_sample.shape
    depth = w_in.shape[0]
    n_pool, page = cache_a_kv.shape[1], cache_a_kv.shape[2]
    wa = N_HEADS * HEAD_DIM
    n_exp = N_GROUPS * EXPERTS_PER_GROUP
    ns = dbsz * nq
    assert nq == 8 and d % LANES == 0 and seq % TOKEN_TILE == 0
    assert page_table.shape[1] % PAGES_PER_STEP == 0

    pool_a = jnp.transpose(cache_a_kv, (0, 1, 3, 4, 5, 2)).reshape(depth, n_pool, 2, wa, page)
    pool_b = jnp.transpose(cache_b_kv, (0, 1, 3, 4, 5, 2)).reshape(depth, n_pool, 2, wa, page)
    pool_f = jnp.transpose(cache_b_logf, (0, 1, 3, 2))
    pool_c = cache_c_kv.reshape(depth, n_pool, page * 2 * N_HEADS, PAIR)

    seg = (jnp.kron(jnp.eye(N_HEADS, dtype=F32), jnp.ones((HEAD_DIM, HEAD_DIM), F32)) / HEAD_DIM).astype(BF16)
    tri_p = jnp.tril(jnp.ones((TOKEN_TILE, TOKEN_TILE), F32)).astype(BF16)
    r = jnp.arange(ns)
    tri_s = jnp.logical_and(r[:, None] >= r[None, :], (r[:, None] // nq) == (r[None, :] // nq)).astype(BF16)
    ltri_a = _strict_lower(STICK_TILE)
    ltri_pg = _strict_lower(PAGES_PER_STEP * page)
    slopes = jnp.asarray(ALIBI_SLOPES, F32)

    hp = x_prompt.reshape(bsz * seq, d)
    hs = x_sample.reshape(ns, d)
    rows_p = [[], [], [], []]
    rows_s = [[], [], [], []]

    for l in range(depth):
        lam_init = 0.8 - 0.6 * math.exp(-0.3 * l)
        lam = (jnp.exp(jnp.sum(lam_q1[l].astype(F32) * lam_k1[l].astype(F32)))
               - jnp.exp(jnp.sum(lam_q2[l].astype(F32) * lam_k2[l].astype(F32))) + lam_init).reshape(1)
        out_scale = 1.0 - lam_init

        w = w_in[l]
        off_f = 6 * wa
        wm = jnp.concatenate([w[:, :off_f], w[:, off_f + N_HEADS:]], axis=1).astype(BF16)
        wf = _pad_lanes(w[:, off_f:off_f + N_HEADS]).astype(BF16)
        bfp = _pad_lanes(b_forget[l].reshape(1, N_HEADS).astype(F32))
        g1 = norm1_g[l].reshape(1, d)
        g2 = norm2_g[l].reshape(1, d)
        gb = jnp.stack([jnp.tile(qn_b_g[l], N_HEADS), jnp.tile(kn_b_g[l], N_HEADS)])
        gc = jnp.stack([jnp.tile(qn_c_g[l], 2 * N_HEADS), jnp.tile(kn_c_g[l], 2 * N_HEADS)])
        gsub = subln_c_g[l].reshape(1, PAIR)
        wr = jnp.concatenate([_pad_lanes(w_router_grp[l])[:, :GATE_LANE0],
                              _pad_lanes(w_router_exp[l], GATE_LANE0)[:, GATE_LANE0:]], axis=1)
        wrh = wr.astype(BF16)
        wrl = (wr - wrh.astype(F32)).astype(BF16)
        br = (_pad_lanes(b_router_grp[l].reshape(1, N_GROUPS))
              + _pad_lanes(b_router_exp[l].reshape(1, n_exp), GATE_LANE0))
        weg = jnp.transpose(w_exp_gate[l], (1, 0, 2)).reshape(d, n_exp * D_EXPERT).astype(BF16)
        weu = jnp.transpose(w_exp_up[l], (1, 0, 2)).reshape(d, n_exp * D_EXPERT).astype(BF16)
        wed = w_exp_down[l].reshape(n_exp * D_EXPERT, d).astype(BF16)
        merge_w = (g1, w_br_a[l].astype(BF16), w_br_b[l].astype(BF16), w_br_c[l].astype(BF16),
                   w_gate[l].astype(BF16), b_gate[l].reshape(1, -1), w_out[l].astype(BF16), g2, wrh, wrl, br)
        proj_w = (g1, wm, wf, bfp, gb, gc, seg)

        arow, brow, crow, lf, cs, abf, bbf, cbf = _proj(hp, tri_p, seq // TOKEN_TILE, *proj_w)
        ck = jnp.transpose(cs.reshape(bsz, seq, LANES)[:, :, :8], (0, 2, 1))
        oa, ob, oc = _prompt_mixers(abf.reshape(bsz, seq, -1), bbf.reshape(bsz, seq, -1),
                                    cbf.reshape(bsz, seq, -1), cs.reshape(bsz, seq, LANES), ck,
                                    ltri_a, slopes, lam, gsub, out_scale)
        h2, x2, gates = _merge(hp, oa.reshape(bsz * seq, -1), ob.reshape(bsz * seq, -1),
                               oc.reshape(bsz * seq, -1), *merge_w, TOKEN_TILE)
        hp = _moe(x2, gates, h2, weg, weu, wed, TOKEN_TILE)
        rows_p[0].append(arow.reshape(bsz, seq, 2, N_HEADS, HEAD_DIM))
        rows_p[1].append(brow.reshape(bsz, seq, 2, N_HEADS, HEAD_DIM))
        rows_p[2].append(lf[:, :N_HEADS].reshape(bsz, seq, N_HEADS))
        rows_p[3].append(crow.reshape(bsz, seq, 2, N_HEADS, 2 * HEAD_DIM))

        arow, brow, crow, lf, cs, abf, bbf, cbf = _proj(hs, tri_s, 1, *proj_w)
        csq = cs.reshape(dbsz, nq, LANES)
        csk = jnp.pad(jnp.transpose(csq[:, :, :8], (0, 2, 1)), ((0, 0), (0, 0), (0, LANES - nq)))
        oa, ob, oc = _sample_mixers(l, page_table, lam, abf.astype(F32).reshape(dbsz, nq, -1),
                                    bbf.astype(F32).reshape(dbsz, nq, -1),
                                    cbf.astype(F32).reshape(dbsz, nq, -1), csq, csk, ltri_pg, gsub,
                                    pool_a, pool_b, pool_f, pool_c, out_scale)
        h2, x2, gates = _merge(hs, oa.reshape(ns, -1), ob.reshape(ns, -1), oc.reshape(ns, -1), *merge_w, ns)
        hs = _moe(x2, gates, h2, weg, weu, wed, ns)
        rows_s[0].append(arow.reshape(dbsz, nq, 2, N_HEADS, HEAD_DIM))
        rows_s[1].append(brow.reshape(dbsz, nq, 2, N_HEADS, HEAD_DIM))
        rows_s[2].append(lf[:, :N_HEADS].reshape(dbsz, nq, N_HEADS))
        rows_s[3].append(crow.reshape(dbsz, nq, 2, N_HEADS, 2 * HEAD_DIM))

    return (hp.reshape(bsz, seq, d), hs.reshape(dbsz, nq, d),
            jnp.stack(rows_p[0]), jnp.stack(rows_p[1]), jnp.stack(rows_p[2]), jnp.stack(rows_p[3]),
            jnp.stack(rows_s[0]), jnp.stack(rows_s[1]), jnp.stack(rows_s[2]), jnp.stack(rows_s[3]))
```

```python
import functools
import math

import jax
import jax.numpy as jnp
from jax import lax
from jax.experimental import pallas as pl
from jax.experimental.pallas import tpu as pltpu

F32 = jnp.float32
BF16 = jnp.bfloat16

HEAD_DIM = 64
HEAD_SHIFT = 6
N_HEADS = 4
PAIR = 2 * HEAD_DIM
N_GROUPS = 4
EXPERTS_PER_GROUP = 8
GROUP_SHIFT = 3
D_EXPERT = 128
NORM_EPS = 1e-6
NEG = -1e30
LANES = 128
GATE_LANE0 = 8
STICK_SKIP = 110.0
SOFTMAX_SKIP = 110.0
ALIBI_SLOPES = tuple(2.0 ** (-8.0 * (h + 1) / N_HEADS) for h in range(N_HEADS))
VMEM_LIMIT = 56 * 1024 * 1024

TOKEN_TILE = 512
STICK_TILE = 256
SOFTMAX_TILE = 512
PAGES_PER_STEP = 8


def _dot(a, b):
    return jnp.dot(a, b, preferred_element_type=F32)


def _dot_nt(a, b):
    return lax.dot_general(a, b, (((1,), (1,)), ((), ())), preferred_element_type=F32)


def _split(x):
    hi = x.astype(BF16)
    lo = (x - hi.astype(F32)).astype(BF16)
    return hi, lo


def _log_sigmoid(x):
    return jnp.minimum(x, 0.0) - jnp.log1p(jnp.exp(-jnp.abs(x)))


def _rms_rows(x, g):
    return x * lax.rsqrt(jnp.mean(x * x, axis=-1, keepdims=True) + NORM_EPS) * g


def _params(sem):
    return pltpu.CompilerParams(dimension_semantics=sem, vmem_limit_bytes=VMEM_LIMIT)


def _iota(shape, dim):
    return lax.broadcasted_iota(jnp.int32, shape, dim)


def _proj_kernel(h_ref, g1_ref, wm_ref, wf_ref, bf_ref, gb_ref, gc_ref, seg_ref, tri_ref,
                 arow_ref, brow_ref, crow_ref, lf_ref, cs_ref, abf_ref, bbf_ref, cbf_ref,
                 carry_ref, *, tiles_per_seq):
    i = pl.program_id(0)
    tm = h_ref.shape[0]
    xb = _rms_rows(h_ref[...], g1_ref[...]).astype(BF16)
    seg = seg_ref[...]
    wa = N_HEADS * HEAD_DIM
    wc = 2 * wa

    def headnorm(u, g):
        hi, lo = _split(u * u)
        ms = _dot(hi, seg) + _dot(lo, seg)
        return u * lax.rsqrt(ms + NORM_EPS) * g

    ua = _dot(xb, wm_ref[:, 0:3 * wa])
    arow_ref[...] = ua[:, wa:3 * wa]
    abf_ref[:, 0:wa] = (ua[:, 0:wa] * 0.125).astype(BF16)
    abf_ref[:, wa:3 * wa] = ua[:, wa:3 * wa].astype(BF16)

    ub = _dot(xb, wm_ref[:, 3 * wa:6 * wa])
    qb = headnorm(ub[:, 0:wa], gb_ref[0:1, :])
    kb = headnorm(ub[:, wa:2 * wa], gb_ref[1:2, :])
    vb = ub[:, 2 * wa:3 * wa]
    brow_ref[:, 0:wa] = kb
    brow_ref[:, wa:2 * wa] = vb
    bbf_ref[:, 0:wa] = (qb * 0.125).astype(BF16)
    bbf_ref[:, wa:2 * wa] = kb.astype(BF16)
    bbf_ref[:, 2 * wa:3 * wa] = vb.astype(BF16)

    c0 = 6 * wa
    uc = _dot(xb, wm_ref[:, c0:c0 + 3 * wc])
    for half in range(2):
        sl = slice(half * wa, (half + 1) * wa)
        qc = headnorm(uc[:, half * wa:(half + 1) * wa], gc_ref[0:1, sl])
        kc = headnorm(uc[:, wc + half * wa:wc + (half + 1) * wa], gc_ref[1:2, sl])
        cbf_ref[:, half * wa:(half + 1) * wa] = (qc * 0.125).astype(BF16)
        cbf_ref[:, wc + half * wa:wc + (half + 1) * wa] = kc.astype(BF16)
        crow_ref[:, half * wa:(half + 1) * wa] = kc
    vc = uc[:, 2 * wc:3 * wc]
    crow_ref[:, wc:2 * wc] = vc
    cbf_ref[:, 2 * wc:3 * wc] = vc.astype(BF16)

    lf = _log_sigmoid(_dot(xb, wf_ref[...]) + bf_ref[...])
    lf_ref[...] = lf

    @pl.when(i % tiles_per_seq == 0)
    def _():
        carry_ref[...] = jnp.zeros_like(carry_ref)

    hi, lo = _split(lf)
    tri = tri_ref[...]
    cs = _dot(tri, hi) + _dot(tri, lo) + carry_ref[...]
    cs_ref[...] = cs
    carry_ref[...] = cs[tm - 1:tm, :]


def _proj(h2d, tri, tiles_per_seq, g1, wm, wf, bfp, gb, gc, seg):
    n, d = h2d.shape
    tm = tri.shape[0]
    wa = N_HEADS * HEAD_DIM
    const = lambda shape: pl.BlockSpec(shape, lambda i: (0,) * len(shape))
    row = lambda w: pl.BlockSpec((tm, w), lambda i: (i, 0))
    out_shapes = [
        jax.ShapeDtypeStruct((n, 2 * wa), F32), jax.ShapeDtypeStruct((n, 2 * wa), F32),
        jax.ShapeDtypeStruct((n, 4 * wa), F32), jax.ShapeDtypeStruct((n, LANES), F32),
        jax.ShapeDtypeStruct((n, LANES), F32), jax.ShapeDtypeStruct((n, 3 * wa), BF16),
        jax.ShapeDtypeStruct((n, 3 * wa), BF16), jax.ShapeDtypeStruct((n, 6 * wa), BF16),
    ]
    return pl.pallas_call(
        functools.partial(_proj_kernel, tiles_per_seq=tiles_per_seq),
        grid=(n // tm,),
        in_specs=[row(d), const((1, d)), const(wm.shape), const(wf.shape), const((1, LANES)),
                  const(gb.shape), const(gc.shape), const(seg.shape), const(tri.shape)],
        out_specs=[row(2 * wa), row(2 * wa), row(4 * wa), row(LANES), row(LANES),
                   row(3 * wa), row(3 * wa), row(6 * wa)],
        out_shape=out_shapes,
        scratch_shapes=[pltpu.VMEM((1, LANES), F32)],
        compiler_params=_params(("arbitrary",)),
        name="proj",
    )(h2d, g1, wm, wf, bfp, gb, gc, seg, tri)


def _head_mask(hh):
    return (_iota((1, PAIR), 1) >> HEAD_SHIFT) == hh


def _store_head(o_ref, hh, hmask, res):
    @pl.when(hh == 0)
    def _():
        o_ref[...] = jnp.where(hmask, res, 0.0).astype(o_ref.dtype)

    @pl.when(hh != 0)
    def _():
        o_ref[...] = jnp.where(hmask, res.astype(o_ref.dtype), o_ref[...])


def _softmax_init(rows, width=PAIR):
    return (jnp.full((rows, 1), NEG, F32), jnp.zeros((rows, 1), F32), jnp.zeros((rows, width), F32))


def _softmax_probs(m, l, s):
    m_new = jnp.maximum(m, jnp.max(s, axis=1, keepdims=True))
    alpha = jnp.exp(m - m_new)
    p = jnp.exp(s - m_new)
    return m_new, alpha * l + jnp.sum(p, axis=1, keepdims=True), alpha, p.astype(BF16)


def _softmax_step(state, s, v):
    m, l, acc = state
    m, l, alpha, p = _softmax_probs(m, l, s)
    return m, l, alpha * acc + _dot(p, v)


def _stick_logs(z, valid=None):
    lb = _log_sigmoid(z)
    ls = lb - z
    if valid is not None:
        ls = jnp.where(valid, ls, 0.0)
    return lb, ls


def _stick_weights(lb, ls, later, run, valid=None):
    w = jnp.exp(lb + later + run)
    if valid is not None:
        w = jnp.where(valid, w, 0.0)
    return w, run + jnp.sum(ls, axis=1, keepdims=True)


def _stick_kernel(q_ref, k_ref, v_ref, ltri_ref, o_ref):
    i = pl.program_id(2)
    hh = pl.program_id(3)
    tq = q_ref.shape[0]
    hmask = _head_mask(hh)
    q = q_ref[...]
    qm = jnp.where(hmask, q, jnp.zeros_like(q))
    ltri = ltri_ref[...]
    valid = _iota((tq, tq), 1) < _iota((tq, tq), 0)

    def block(j, run, acc, mask):
        off = pl.multiple_of(j * tq, tq)
        lb, ls = _stick_logs(_dot_nt(qm, k_ref[pl.ds(off, tq), :]), mask)
        hi, lo = _split(ls)
        w, run = _stick_weights(lb, ls, _dot(hi, ltri) + _dot(lo, ltri), run, mask)
        return run, acc + _dot(w.astype(BF16), v_ref[pl.ds(off, tq), :])

    run, acc = block(i, jnp.zeros((tq, 1), F32), jnp.zeros((tq, PAIR), F32), valid)

    def cond(c):
        j, run, _ = c
        return jnp.logical_and(j >= 0, jnp.max(run) > -STICK_SKIP)

    def body(c):
        j, run, acc = c
        run, acc = block(j, run, acc, None)
        return j - 1, run, acc

    _, _, acc = lax.while_loop(cond, body, (i - 1, run, acc))
    _store_head(o_ref, hh, hmask, acc)


def _fox_kernel(reach_ref, q_ref, k_ref, v_ref, cq_ref, ck_ref, o_ref):
    pair = pl.program_id(1)
    i = pl.program_id(2)
    hh = pl.program_id(3)
    h = pair * 2 + hh
    tq = q_ref.shape[0]
    hmask = _head_mask(hh)
    q = q_ref[...]
    qm = jnp.where(hmask, q, jnp.zeros_like(q))
    cq = jnp.sum(jnp.where(_iota((1, LANES), 1) == h, cq_ref[...], 0.0), axis=1, keepdims=True)
    causal = _iota((tq, tq), 1) <= _iota((tq, tq), 0)

    def scores(j, diag):
        off = pl.multiple_of(j * tq, tq)
        ck = ck_ref[pl.ds(h, 1), pl.ds(off, tq)]
        s = _dot_nt(qm, k_ref[pl.ds(off, tq), :]) + (cq - ck)
        return jnp.where(causal, s, NEG) if diag else s

    def update(state, s, j):
        return _softmax_step(state, s, v_ref[pl.ds(pl.multiple_of(j * tq, tq), tq), :])

    def body(t, c):
        s, state = c
        j = i - 1 - t
        return scores(j, False), update(state, s, j + 1)

    n_old = reach_ref[pl.program_id(0), h, i]
    s, state = lax.fori_loop(0, n_old, body, (scores(i, True), _softmax_init(tq)))
    _, l, acc = update(state, s, i - n_old)
    _store_head(o_ref, hh, hmask, acc / l)


def _diff_kernel(slope_ref, lam_ref, reach_ref, q_ref, k_ref, v_ref, g_ref, o_ref, *, out_scale):
    h = pl.program_id(1)
    i = pl.program_id(2)
    tq = q_ref.shape[0]
    slope = slope_ref[h]
    lam = lam_ref[0]
    q = q_ref[...]
    qms = [jnp.where(_head_mask(c), q, jnp.zeros_like(q)) for c in range(2)]
    row = _iota((tq, tq), 0)
    col = _iota((tq, tq), 1)
    causal = col <= row
    sd0 = slope * (row - col).astype(F32)

    def scores(j, diag):
        k = k_ref[pl.ds(pl.multiple_of(j * tq, tq), tq), :]
        sb = sd0 + slope * ((i - j) * tq).astype(F32)
        out = []
        for c in range(2):
            s = _dot_nt(qms[c], k) - sb
            out.append(jnp.where(causal, s, NEG) if diag else s)
        return tuple(out)

    def update(states, ss, j):
        v = v_ref[pl.ds(pl.multiple_of(j * tq, tq), tq), :]
        return tuple(_softmax_step(states[c], ss[c], v) for c in range(2))

    def body(t, c):
        ss, states = c
        j = i - 1 - t
        return scores(j, False), update(states, ss, j + 1)

    n_old = jnp.minimum(i, reach_ref[h])
    ss, states = lax.fori_loop(0, n_old, body, (scores(i, True), (_softmax_init(tq), _softmax_init(tq))))
    (_, l0, a0), (_, l1, a1) = update(states, ss, i - n_old)
    o = a0 / l0 - lam * (a1 / l1)
    o_ref[...] = (_rms_rows(o, g_ref[...]) * out_scale).astype(o_ref.dtype)


def _qk_bound(gq, gk):
    return 1.02 * math.sqrt(HEAD_DIM) * jnp.max(jnp.abs(gq)) * jnp.max(jnp.abs(gk))


def _fox_reach(cs, bound, tq):
    c = cs[:, :, :N_HEADS]
    c_first = c[:, 0::tq, :]
    c_last = c[:, tq - 1::tq, :]
    nt = c_first.shape[1]
    i = jnp.arange(nt)[:, None]
    j = jnp.arange(nt)[None, :]
    bias = c_first[:, :, None, :] - c_last[:, None, :, :]
    need = jnp.logical_and(2.0 * bound + bias >= -SOFTMAX_SKIP, (j < i)[None, :, :, None])
    reach = jnp.max(jnp.where(need, (i - j)[None, :, :, None], 0), axis=2)
    return jnp.transpose(reach, (0, 2, 1)).astype(jnp.int32)


def _alibi_reach(bound, tq, n_tiles):
    dist = (2.0 * bound + SOFTMAX_SKIP) / jnp.asarray(ALIBI_SLOPES, F32)
    return jnp.clip(jnp.floor((dist - 1.0) / tq) + 1.0, 0, n_tiles).astype(jnp.int32)


def _prompt_mixers(abf, bbf, cbf, cs, ck, ltri, slopes, lam, gsub, out_scale, bound_b, bound_c):
    bsz, t, _ = abf.shape
    npair = N_HEADS // 2
    wa = N_HEADS * HEAD_DIM

    def pair_specs(tile):
        qs = pl.BlockSpec((None, tile, PAIR), lambda b, p, i, hh: (b, i, p))
        ks = pl.BlockSpec((None, t, PAIR), lambda b, p, i, hh: (b, 0, npair + p))
        vs = pl.BlockSpec((None, t, PAIR), lambda b, p, i, hh: (b, 0, 2 * npair + p))
        os = pl.BlockSpec((None, tile, PAIR), lambda b, p, i, hh: (b, i, p))
        return qs, ks, vs, os

    sem4 = ("arbitrary",) * 4
    qs, ks, vs, os = pair_specs(STICK_TILE)
    oa = pl.pallas_call(
        _stick_kernel,
        grid=(bsz, npair, t // STICK_TILE, 2),
        in_specs=[qs, ks, vs, pl.BlockSpec(ltri.shape, lambda b, p, i, hh: (0, 0))],
        out_specs=os,
        out_shape=jax.ShapeDtypeStruct((bsz, t, wa), BF16),
        compiler_params=_params(sem4),
        name="stick_prompt",
    )(abf, abf, abf, ltri)

    tq = SOFTMAX_TILE
    smem = pl.BlockSpec(memory_space=pltpu.SMEM)
    qs, ks, vs, os = pair_specs(tq)
    ob = pl.pallas_call(
        _fox_kernel,
        grid=(bsz, npair, t // tq, 2),
        in_specs=[smem, qs, ks, vs,
                  pl.BlockSpec((None, tq, LANES), lambda b, p, i, hh: (b, i, 0)),
                  pl.BlockSpec((None, 8, t), lambda b, p, i, hh: (b, 0, 0))],
        out_specs=os,
        out_shape=jax.ShapeDtypeStruct((bsz, t, wa), BF16),
        compiler_params=_params(sem4),
        name="fox_prompt",
    )(_fox_reach(cs, bound_b, tq), bbf, bbf, bbf, cs, ck)

    oc = pl.pallas_call(
        functools.partial(_diff_kernel, out_scale=out_scale),
        grid=(bsz, N_HEADS, t // tq),
        in_specs=[smem, smem, smem,
                  pl.BlockSpec((None, tq, PAIR), lambda b, h, i: (b, i, h)),
                  pl.BlockSpec((None, t, PAIR), lambda b, h, i: (b, 0, N_HEADS + h)),
                  pl.BlockSpec((None, t, PAIR), lambda b, h, i: (b, 0, 2 * N_HEADS + h)),
                  pl.BlockSpec((1, PAIR), lambda b, h, i: (0, 0))],
        out_specs=pl.BlockSpec((None, tq, PAIR), lambda b, h, i: (b, i, h)),
        out_shape=jax.ShapeDtypeStruct((bsz, t, 2 * wa), BF16),
        compiler_params=_params(("arbitrary",) * 3),
        name="diff_prompt",
    )(slopes, lam, _alibi_reach(bound_c, tq, t // tq), cbf, cbf, cbf, gsub)
    return oa, ob, oc


def _sample_kernel(pt_ref, lam_ref, qa_ref, qb_ref, qc_ref, csq_ref, csk_ref, ltri_ref, g_ref, *rest,
                   pages_per_step, n_pages, out_scale):
    del pt_ref
    gsz = pages_per_step
    page_refs = rest[:4 * gsz]
    oa_ref, ob_ref, oc_ref = rest[4 * gsz:4 * gsz + 3]
    (ra_ref, acca_ref, mb_ref, lb_ref, accb_ref, carry_ref,
     mc_ref, lc_ref, accc_ref) = rest[4 * gsz + 3:]
    j = pl.program_id(1)
    nq = qa_ref.shape[0]
    span = ltri_ref.shape[0]
    page = span // gsz
    wa = N_HEADS * HEAD_DIM
    wc = 2 * wa
    ra_rows = N_HEADS * nq
    rc_rows = 2 * ra_rows
    nq_shift = nq.bit_length() - 1
    lam = lam_ref[0]

    def stacked_q(x, reps):
        return jnp.concatenate([x] * reps, axis=0)

    def pad_rows(x):
        return jnp.concatenate([x, jnp.zeros((page - nq, x.shape[1]), x.dtype)], axis=0)

    def expand_heads(x):
        return jnp.concatenate(
            [jnp.broadcast_to(x[h:h + 1, :], (nq, x.shape[1])) for h in range(N_HEADS)], axis=0)

    qa = qa_ref[...]
    qb = qb_ref[...]
    qc = qc_ref[...]
    diag_a = (_iota((ra_rows, wa), 1) >> HEAD_SHIFT) == (_iota((ra_rows, wa), 0) >> nq_shift)
    diag_c = (_iota((rc_rows, wc), 1) >> HEAD_SHIFT) == (_iota((rc_rows, wc), 0) >> nq_shift)
    qma = jnp.where(diag_a, stacked_q(qa[:, 0:wa], N_HEADS), 0.0).astype(BF16)
    qmb = jnp.where(diag_a, stacked_q(qb[:, 0:wa], N_HEADS), 0.0).astype(BF16)
    qmc = jnp.where(diag_c, stacked_q(qc[:, 0:wc], 2 * N_HEADS), 0.0).astype(BF16)
    head_c = _iota((rc_rows, 1), 0) >> (nq_shift + 1)
    slope_c = jnp.full((rc_rows, 1), ALIBI_SLOPES[N_HEADS - 1], F32)
    for h in range(N_HEADS - 1):
        slope_c = jnp.where(head_c == h, ALIBI_SLOPES[h], slope_c)

    csq = csq_ref[...]
    lane8 = _iota((nq, LANES), 1)
    fnew = jnp.concatenate(
        [jnp.sum(jnp.where(lane8 == h, csq, 0.0), axis=1, keepdims=True) for h in range(N_HEADS)],
        axis=0)

    @pl.when(j == 0)
    def _():
        lane_k = _iota((ra_rows, page), 1)
        qi_a = _iota((ra_rows, page), 0) & (nq - 1)
        ltri = ltri_ref[0:page, 0:page]
        ka = pad_rows(qa[:, wa:2 * wa]).astype(BF16)
        va = pad_rows(qa[:, 2 * wa:3 * wa]).astype(BF16)
        lb, ls = _stick_logs(_dot_nt(qma, ka), lane_k < qi_a)
        hi, lo = _split(ls)
        w, run = _stick_weights(lb, ls, _dot(hi, ltri) + _dot(lo, ltri), jnp.zeros((ra_rows, 1), F32),
                                lane_k < qi_a)
        ra_ref[...] = run
        acca_ref[...] = _dot(w.astype(BF16), va)

        kb = pad_rows(qb[:, wa:2 * wa]).astype(BF16)
        vb = pad_rows(qb[:, 2 * wa:3 * wa]).astype(BF16)
        s = _dot_nt(qmb, kb) + (fnew - expand_heads(csk_ref[...]))
        m, l, acc = _softmax_step(_softmax_init(ra_rows, wa), jnp.where(lane_k <= qi_a, s, NEG), vb)
        mb_ref[...] = m
        lb_ref[...] = l
        accb_ref[...] = acc
        carry_ref[...] = jnp.zeros_like(carry_ref)

        lane_kc = _iota((rc_rows, page), 1)
        qi_c = _iota((rc_rows, page), 0) & (nq - 1)
        kc = pad_rows(qc[:, wc:2 * wc]).astype(BF16)
        vc = pad_rows(qc[:, 2 * wc:3 * wc]).astype(BF16)
        s = _dot_nt(qmc, kc) - slope_c * (qi_c - lane_kc).astype(F32)
        m, l, acc = _softmax_step(_softmax_init(rc_rows, wc), jnp.where(lane_kc <= qi_c, s, NEG), vc)
        mc_ref[...] = m
        lc_ref[...] = l
        accc_ref[...] = acc

    a_refs = page_refs[0::4]
    b_refs = page_refs[1::4]
    f_refs = page_refs[2::4]
    c_refs = page_refs[3::4]
    first_page = n_pages - (j + 1) * gsz
    ltri = ltri_ref[...]

    def lanes_cat(refs, idx):
        return jnp.concatenate([r[idx].astype(BF16) for r in refs], axis=1)

    def rows_cat(kv):
        return jnp.concatenate(
            [jnp.concatenate([r[pl.ds(kv * N_HEADS + h, page, stride=2 * N_HEADS), :].astype(BF16)
                              for h in range(N_HEADS)], axis=1) for r in c_refs], axis=0)

    z_a = _dot(qma, lanes_cat(a_refs, 0))
    s_b = _dot(qmb, lanes_cat(b_refs, 0))
    s_c = _dot_nt(qmc, rows_cat(0))

    lb, ls = _stick_logs(z_a)
    lf_r = expand_heads(jnp.concatenate([r[...] for r in f_refs], axis=1))
    cum = _dot(jnp.concatenate(_split(ls) + _split(lf_r), axis=0), ltri)
    later = cum[0:ra_rows] + cum[ra_rows:2 * ra_rows]
    sfx_r = cum[2 * ra_rows:3 * ra_rows] + cum[3 * ra_rows:4 * ra_rows]
    w, run = _stick_weights(lb, ls, later, ra_ref[...])
    ra_ref[...] = run

    carry = carry_ref[...]
    m_b, l_b, alpha_b, p_b = _softmax_probs(mb_ref[...], lb_ref[...], s_b + (sfx_r + (fnew + carry)))
    mb_ref[...] = m_b
    lb_ref[...] = l_b
    carry_ref[...] = carry + (sfx_r[:, 0:1] + lf_r[:, 0:1])

    qi_c = _iota((rc_rows, span), 0) & (nq - 1)
    dist = (qi_c - _iota((rc_rows, span), 1) + (n_pages - first_page) * page).astype(F32)
    m_c, l_c, alpha_c, p_c = _softmax_probs(mc_ref[...], lc_ref[...], s_c - slope_c * dist)
    mc_ref[...] = m_c
    lc_ref[...] = l_c

    acca_ref[...] = acca_ref[...] + _dot_nt(w.astype(BF16), lanes_cat(a_refs, 1))
    accb_ref[...] = alpha_b * accb_ref[...] + _dot_nt(p_b, lanes_cat(b_refs, 1))
    accc_ref[...] = alpha_c * accc_ref[...] + _dot(p_c, rows_cat(1))

    @pl.when(j == pl.num_programs(1) - 1)
    def _():
        lane_o = _iota((nq, wa), 1) >> HEAD_SHIFT
        acc_a = acca_ref[...]
        nb = accb_ref[...] / lb_ref[...]
        oa = jnp.zeros((nq, wa), F32)
        ob = jnp.zeros((nq, wa), F32)
        for h in range(N_HEADS):
            oa = jnp.where(lane_o == h, acc_a[h * nq:(h + 1) * nq, :], oa)
            ob = jnp.where(lane_o == h, nb[h * nq:(h + 1) * nq, :], ob)
        oa_ref[...] = oa
        ob_ref[...] = ob
        nc = accc_ref[...] / lc_ref[...]
        for h in range(N_HEADS):
            blk = nc[2 * h * nq:2 * (h + 1) * nq, h * PAIR:(h + 1) * PAIR]
            o = blk[0:nq, :] - lam * blk[nq:2 * nq, :]
            oc_ref[:, h * PAIR:(h + 1) * PAIR] = _rms_rows(o, g_ref[...]) * out_scale


def _sample_mixers(layer, page_table, lam, abf, bbf, cbf, csq, csk, ltri, gsub,
                   pool_a, pool_b, pool_f, pool_c, out_scale):
    dbsz, nq, _ = abf.shape
    n_pages = page_table.shape[1]
    gsz = PAGES_PER_STEP
    page = ltri.shape[0] // gsz
    wa = N_HEADS * HEAD_DIM
    steps = n_pages // gsz

    def seq_spec(w):
        return pl.BlockSpec((None, nq, w), lambda b, j, pt: (b, 0, 0))

    def page_specs(g):
        def idx(b, j, pt):
            return pt[b, n_pages - (j + 1) * gsz + g]
        return [
            pl.BlockSpec((None, None, 2, wa, page), lambda b, j, pt: (layer, idx(b, j, pt), 0, 0, 0)),
            pl.BlockSpec((None, None, 2, wa, page), lambda b, j, pt: (layer, idx(b, j, pt), 0, 0, 0)),
            pl.BlockSpec((None, None, N_HEADS, page), lambda b, j, pt: (layer, idx(b, j, pt), 0, 0)),
            pl.BlockSpec((None, None, 2 * N_HEADS * page, PAIR),
                         lambda b, j, pt: (layer, idx(b, j, pt), 0, 0)),
        ]

    in_specs = [pl.BlockSpec(memory_space=pltpu.SMEM), seq_spec(3 * wa), seq_spec(3 * wa), seq_spec(6 * wa),
                seq_spec(LANES), pl.BlockSpec((None, 8, LANES), lambda b, j, pt: (b, 0, 0)),
                pl.BlockSpec(ltri.shape, lambda b, j, pt: (0, 0)),
                pl.BlockSpec((1, PAIR), lambda b, j, pt: (0, 0))]
    pools = []
    for g in range(gsz):
        in_specs += page_specs(g)
        pools += [pool_a, pool_b, pool_f, pool_c]
    ra_rows = N_HEADS * nq
    rc_rows = 2 * ra_rows
    scratch = [pltpu.VMEM((ra_rows, 1), F32), pltpu.VMEM((ra_rows, wa), F32),
               pltpu.VMEM((ra_rows, 1), F32), pltpu.VMEM((ra_rows, 1), F32), pltpu.VMEM((ra_rows, wa), F32),
               pltpu.VMEM((ra_rows, 1), F32),
               pltpu.VMEM((rc_rows, 1), F32), pltpu.VMEM((rc_rows, 1), F32), pltpu.VMEM((rc_rows, 2 * wa), F32)]
    return pl.pallas_call(
        functools.partial(_sample_kernel, pages_per_step=gsz, n_pages=n_pages, out_scale=out_scale),
        grid_spec=pltpu.PrefetchScalarGridSpec(
            num_scalar_prefetch=1,
            grid=(dbsz, steps),
            in_specs=in_specs,
            out_specs=[seq_spec(wa), seq_spec(wa), seq_spec(2 * wa)],
            scratch_shapes=scratch),
        out_shape=[jax.ShapeDtypeStruct((dbsz, nq, wa), F32), jax.ShapeDtypeStruct((dbsz, nq, wa), F32),
                   jax.ShapeDtypeStruct((dbsz, nq, 2 * wa), F32)],
        compiler_params=_params(("arbitrary", "arbitrary")),
        name="sample_mixers",
    )(page_table, lam, abf, bbf, cbf, csq, csk, ltri, gsub, *pools)


def _merge_kernel(h_ref, oa_ref, ob_ref, oc_ref, g1_ref, wa_ref, wb_ref, wc_ref, wg_ref, bg_ref, wo_ref,
                  g2_ref, wrh_ref, wrl_ref, br_ref, h2_ref, x2_ref, gates_ref):
    x = h_ref[...]
    d = x.shape[1]
    xb = _rms_rows(x, g1_ref[...]).astype(BF16)
    ys = (_dot(oa_ref[...].astype(BF16), wa_ref[...]),
          _dot(ob_ref[...].astype(BF16), wb_ref[...]),
          _dot(oc_ref[...].astype(BF16), wc_ref[...]))
    merged = jnp.zeros_like(x)
    for idx, y in enumerate(ys):
        gl = _dot(xb, wg_ref[:, idx * d:(idx + 1) * d]) + bg_ref[:, idx * d:(idx + 1) * d]
        merged = merged + y / (1.0 + jnp.exp(-gl))
    h2 = x + _dot(merged.astype(BF16), wo_ref[...])
    h2_ref[...] = h2
    x2 = _rms_rows(h2, g2_ref[...])
    x2_ref[...] = x2.astype(BF16)

    xh, xl = _split(x2)
    wrh = wrh_ref[...]
    logits = _dot(xh, wrh) + _dot(xl, wrh) + _dot(xh, wrl_ref[...]) + br_ref[...]
    lane = _iota((1, LANES), 1)
    lane_f = lane.astype(F32)
    n_exp = N_GROUPS * EXPERTS_PER_GROUP
    is_exp = jnp.logical_and(lane >= GATE_LANE0, lane < GATE_LANE0 + n_exp)
    grp_of_lane = jnp.where(is_exp, ((lane - GATE_LANE0) >> GROUP_SHIFT).astype(F32), -1.0)

    def first_max(vals):
        mx = jnp.max(vals, axis=1, keepdims=True)
        idx = jnp.min(jnp.where(vals == mx, lane_f, float(LANES)), axis=1, keepdims=True)
        return mx, idx

    glog = jnp.where(lane < N_GROUPS, logits, NEG)
    gmax, gidx = first_max(glog)
    p_g = 1.0 / jnp.sum(jnp.exp(glog - gmax), axis=1, keepdims=True)
    elog = jnp.where(grp_of_lane == gidx, logits, NEG)
    m1, i1 = first_max(elog)
    elog2 = jnp.where(lane_f == i1, NEG, elog)
    m2, i2 = first_max(elog2)
    e2 = jnp.exp(m2 - m1)
    w1 = p_g / (1.0 + e2)
    w2 = p_g * e2 / (1.0 + e2)
    gates_ref[...] = jnp.where(lane_f == i1, w1, 0.0) + jnp.where(lane_f == i2, w2, 0.0)


def _merge(h2d, oa, ob, oc, g1, wa, wb, wc, wg, bg, wo, g2, wrh, wrl, br, tm):
    n, d = h2d.shape
    const = lambda a: pl.BlockSpec(a.shape, lambda i: (0,) * a.ndim)
    row = lambda w: pl.BlockSpec((tm, w), lambda i: (i, 0))
    return pl.pallas_call(
        _merge_kernel,
        grid=(n // tm,),
        in_specs=[row(d), row(oa.shape[1]), row(ob.shape[1]), row(oc.shape[1]), const(g1), const(wa),
                  const(wb), const(wc), const(wg), const(bg), const(wo), const(g2), const(wrh),
                  const(wrl), const(br)],
        out_specs=[row(d), row(d), row(LANES)],
        out_shape=[jax.ShapeDtypeStruct((n, d), F32), jax.ShapeDtypeStruct((n, d), BF16),
                   jax.ShapeDtypeStruct((n, LANES), F32)],
        compiler_params=_params(("arbitrary",)),
        name="merge",
    )(h2d, oa, ob, oc, g1, wa, wb, wc, wg, bg, wo, g2, wrh, wrl, br)


def _moe_kernel(x2_ref, gates_ref, h2_ref, wg_ref, wu_ref, wd_ref, o_ref):
    g = pl.program_id(1)
    x = x2_ref[...]
    hg = _dot(x, wg_ref[...])
    hu = _dot(x, wu_ref[...])
    act = hg / (1.0 + jnp.exp(-hg)) * hu
    gates = gates_ref[...]
    lane = _iota((1, LANES), 1)
    parts = []
    for e in range(EXPERTS_PER_GROUP):
        col = jnp.sum(jnp.where(lane == GATE_LANE0 + g * EXPERTS_PER_GROUP + e, gates, 0.0),
                      axis=1, keepdims=True)
        parts.append((act[:, e * D_EXPERT:(e + 1) * D_EXPERT] * col).astype(BF16))
    y = _dot(jnp.concatenate(parts, axis=1), wd_ref[...])

    @pl.when(g == 0)
    def _():
        o_ref[...] = h2_ref[...] + y

    @pl.when(g != 0)
    def _():
        o_ref[...] = o_ref[...] + y


def _moe(x2, gates, h2, wg, wu, wd, tm):
    n, d = h2.shape
    gw = EXPERTS_PER_GROUP * D_EXPERT
    return pl.pallas_call(
        _moe_kernel,
        grid=(n // tm, N_GROUPS),
        in_specs=[pl.BlockSpec((tm, d), lambda i, g: (i, 0)), pl.BlockSpec((tm, LANES), lambda i, g: (i, 0)),
                  pl.BlockSpec((tm, d), lambda i, g: (i, 0)), pl.BlockSpec((d, gw), lambda i, g: (0, g)),
                  pl.BlockSpec((d, gw), lambda i, g: (0, g)), pl.BlockSpec((gw, d), lambda i, g: (g, 0))],
        out_specs=pl.BlockSpec((tm, d), lambda i, g: (i, 0)),
        out_shape=jax.ShapeDtypeStruct((n, d), F32),
        compiler_params=_params(("arbitrary", "arbitrary")),
        name="moe",
    )(x2, gates, h2, wg, wu, wd)


def _pad_lanes(x, lane0=0):
    return jnp.pad(x, ((0, 0), (lane0, LANES - lane0 - x.shape[1])))


def _strict_lower(n):
    return jnp.tril(jnp.ones((n, n), F32), -1).astype(BF16)


def kernel(x_prompt, x_sample, cache_a_kv, cache_b_kv, cache_b_logf, cache_c_kv, page_table, norm1_g, w_in, b_forget, qn_b_g, kn_b_g, qn_c_g, kn_c_g, lam_q1, lam_k1, lam_q2, lam_k2, subln_c_g, w_br_a, w_br_b, w_br_c, w_gate, b_gate, w_out, norm2_g, w_router_grp, b_router_grp, w_router_exp, b_router_exp, w_exp_gate, w_exp_up, w_exp_down):
    bsz, seq, d = x_prompt.shape
    dbsz, nq, _ = x_sample.shape
    depth = w_in.shape[0]
    n_pool, page = cache_a_kv.shape[1], cache_a_kv.shape[2]
    wa = N_HEADS * HEAD_DIM
    n_exp = N_GROUPS * EXPERTS_PER_GROUP
    ns = dbsz * nq
    assert nq == 8 and d % LANES == 0 and seq % TOKEN_TILE == 0
    assert page_table.shape[1] % PAGES_PER_STEP == 0

    pool_a = jnp.transpose(cache_a_kv, (0, 1, 3, 4, 5, 2)).reshape(depth, n_pool, 2, wa, page)
    pool_b = jnp.transpose(cache_b_kv, (0, 1, 3, 4, 5, 2)).reshape(depth, n_pool, 2, wa, page)
    pool_f = jnp.transpose(cache_b_logf, (0, 1, 3, 2))
    pool_c = cache_c_kv.reshape(depth, n_pool, page * 2 * N_HEADS, PAIR)

    seg = (jnp.kron(jnp.eye(N_HEADS, dtype=F32), jnp.ones((HEAD_DIM, HEAD_DIM), F32)) / HEAD_DIM).astype(BF16)
    tri_p = jnp.tril(jnp.ones((TOKEN_TILE, TOKEN_TILE), F32)).astype(BF16)
    r = jnp.arange(ns)
    tri_s = jnp.logical_and(r[:, None] >= r[None, :], (r[:, None] // nq) == (r[None, :] // nq)).astype(BF16)
    ltri_a = _strict_lower(STICK_TILE)
    ltri_pg = _strict_lower(PAGES_PER_STEP * page)
    slopes = jnp.asarray(ALIBI_SLOPES, F32)

    hp = x_prompt.reshape(bsz * seq, d)
    hs = x_sample.reshape(ns, d)
    rows_p = [[], [], [], []]
    rows_s = [[], [], [], []]

    for l in range(depth):
        lam_init = 0.8 - 0.6 * math.exp(-0.3 * l)
        lam = (jnp.exp(jnp.sum(lam_q1[l].astype(F32) * lam_k1[l].astype(F32)))
               - jnp.exp(jnp.sum(lam_q2[l].astype(F32) * lam_k2[l].astype(F32))) + lam_init).reshape(1)
        out_scale = 1.0 - lam_init

        w = w_in[l]
        off_f = 6 * wa
        wm = jnp.concatenate([w[:, :off_f], w[:, off_f + N_HEADS:]], axis=1).astype(BF16)
        wf = _pad_lanes(w[:, off_f:off_f + N_HEADS]).astype(BF16)
        bfp = _pad_lanes(b_forget[l].reshape(1, N_HEADS).astype(F32))
        g1 = norm1_g[l].reshape(1, d)
        g2 = norm2_g[l].reshape(1, d)
        gb = jnp.stack([jnp.tile(qn_b_g[l], N_HEADS), jnp.tile(kn_b_g[l], N_HEADS)])
        gc = jnp.stack([jnp.tile(qn_c_g[l], 2 * N_HEADS), jnp.tile(kn_c_g[l], 2 * N_HEADS)])
        gsub = subln_c_g[l].reshape(1, PAIR)
        wr = jnp.concatenate([_pad_lanes(w_router_grp[l])[:, :GATE_LANE0],
                              _pad_lanes(w_router_exp[l], GATE_LANE0)[:, GATE_LANE0:]], axis=1)
        wrh = wr.astype(BF16)
        wrl = (wr - wrh.astype(F32)).astype(BF16)
        br = (_pad_lanes(b_router_grp[l].reshape(1, N_GROUPS))
              + _pad_lanes(b_router_exp[l].reshape(1, n_exp), GATE_LANE0))
        weg = jnp.transpose(w_exp_gate[l], (1, 0, 2)).reshape(d, n_exp * D_EXPERT).astype(BF16)
        weu = jnp.transpose(w_exp_up[l], (1, 0, 2)).reshape(d, n_exp * D_EXPERT).astype(BF16)
        wed = w_exp_down[l].reshape(n_exp * D_EXPERT, d).astype(BF16)
        merge_w = (g1, w_br_a[l].astype(BF16), w_br_b[l].astype(BF16), w_br_c[l].astype(BF16),
                   w_gate[l].astype(BF16), b_gate[l].reshape(1, -1), w_out[l].astype(BF16), g2, wrh, wrl, br)
        proj_w = (g1, wm, wf, bfp, gb, gc, seg)

        arow, brow, crow, lf, cs, abf, bbf, cbf = _proj(hp, tri_p, seq // TOKEN_TILE, *proj_w)
        ck = jnp.transpose(cs.reshape(bsz, seq, LANES)[:, :, :8], (0, 2, 1))
        oa, ob, oc = _prompt_mixers(abf.reshape(bsz, seq, -1), bbf.reshape(bsz, seq, -1),
                                    cbf.reshape(bsz, seq, -1), cs.reshape(bsz, seq, LANES), ck,
                                    ltri_a, slopes, lam, gsub, out_scale,
                                    _qk_bound(qn_b_g[l], kn_b_g[l]), _qk_bound(qn_c_g[l], kn_c_g[l]))
        h2, x2, gates = _merge(hp, oa.reshape(bsz * seq, -1), ob.reshape(bsz * seq, -1),
                               oc.reshape(bsz * seq, -1), *merge_w, TOKEN_TILE)
        hp = _moe(x2, gates, h2, weg, weu, wed, TOKEN_TILE)
        rows_p[0].append(arow.reshape(bsz, seq, 2, N_HEADS, HEAD_DIM))
        rows_p[1].append(brow.reshape(bsz, seq, 2, N_HEADS, HEAD_DIM))
        rows_p[2].append(lf[:, :N_HEADS].reshape(bsz, seq, N_HEADS))
        rows_p[3].append(crow.reshape(bsz, seq, 2, N_HEADS, 2 * HEAD_DIM))

        arow, brow, crow, lf, cs, abf, bbf, cbf = _proj(hs, tri_s, 1, *proj_w)
        csq = cs.reshape(dbsz, nq, LANES)
        csk = jnp.pad(jnp.transpose(csq[:, :, :8], (0, 2, 1)), ((0, 0), (0, 0), (0, LANES - nq)))
        oa, ob, oc = _sample_mixers(l, page_table, lam, abf.astype(F32).reshape(dbsz, nq, -1),
                                    bbf.astype(F32).reshape(dbsz, nq, -1),
                                    cbf.astype(F32).reshape(dbsz, nq, -1), csq, csk, ltri_pg, gsub,
                                    pool_a, pool_b, pool_f, pool_c, out_scale)
        h2, x2, gates = _merge(hs, oa.reshape(ns, -1), ob.reshape(ns, -1), oc.reshape(ns, -1), *merge_w, ns)
        hs = _moe(x2, gates, h2, weg, weu, wed, ns)
        rows_s[0].append(arow.reshape(dbsz, nq, 2, N_HEADS, HEAD_DIM))
        rows_s[1].append(brow.reshape(dbsz, nq, 2, N_HEADS, HEAD_DIM))
        rows_s[2].append(lf[:, :N_HEADS].reshape(dbsz, nq, N_HEADS))
        rows_s[3].append(crow.reshape(dbsz, nq, 2, N_HEADS, 2 * HEAD_DIM))

    return (hp.reshape(bsz, seq, d), hs.reshape(dbsz, nq, d),
            jnp.stack(rows_p[0]), jnp.stack(rows_p[1]), jnp.stack(rows_p[2]), jnp.stack(rows_p[3]),
            jnp.stack(rows_s[0]), jnp.stack(rows_s[1]), jnp.stack(rows_s[2]), jnp.stack(rows_s[3]))
```

```python
import functools
import math

import jax
import jax.numpy as jnp
from jax import lax
from jax.experimental import pallas as pl
from jax.experimental.pallas import tpu as pltpu

F32 = jnp.float32
BF16 = jnp.bfloat16

HEAD_DIM = 64
HEAD_SHIFT = 6
N_HEADS = 4
PAIR = 2 * HEAD_DIM
N_GROUPS = 4
EXPERTS_PER_GROUP = 8
GROUP_SHIFT = 3
D_EXPERT = 128
NORM_EPS = 1e-6
NEG = -1e30
LANES = 128
GATE_LANE0 = 8
STICK_SKIP = 110.0
SOFTMAX_SKIP = 110.0
ALIBI_SLOPES = tuple(2.0 ** (-8.0 * (h + 1) / N_HEADS) for h in range(N_HEADS))
VMEM_LIMIT = 56 * 1024 * 1024

TOKEN_TILE = 512
STICK_TILE = 256
SOFTMAX_TILE = 512
PAGES_PER_STEP = 8


def _dot(a, b):
    return jnp.dot(a, b, preferred_element_type=F32)


def _dot_nt(a, b):
    return lax.dot_general(a, b, (((1,), (1,)), ((), ())), preferred_element_type=F32)


def _split(x):
    hi = x.astype(BF16)
    lo = (x - hi.astype(F32)).astype(BF16)
    return hi, lo


def _log_sigmoid(x):
    return jnp.minimum(x, 0.0) - jnp.log1p(jnp.exp(-jnp.abs(x)))


def _rms_rows(x, g):
    return x * lax.rsqrt(jnp.mean(x * x, axis=-1, keepdims=True) + NORM_EPS) * g


def _params(sem):
    return pltpu.CompilerParams(dimension_semantics=sem, vmem_limit_bytes=VMEM_LIMIT)


def _iota(shape, dim):
    return lax.broadcasted_iota(jnp.int32, shape, dim)


def _proj_kernel(h_ref, g1_ref, wab_ref, wc_ref, wf_ref, bf_ref, gb_ref, gc_ref, seg_ref, tri_ref, *rest,
                 tiles_per_seq):
    (arow_ref, brow_ref, crow_ref, lf_ref, cs_ref, abf_ref, bbf_ref, cbf_ref, carry_ref) = rest[-9:]
    i = pl.program_id(0)
    tm = h_ref.shape[0]
    xb = _rms_rows(h_ref[...], g1_ref[...]).astype(BF16)
    seg = seg_ref[...]
    wa = N_HEADS * HEAD_DIM
    wc = 2 * wa

    def headnorm(u, g):
        hi, lo = _split(u * u)
        ms = _dot(hi, seg) + _dot(lo, seg)
        return u * lax.rsqrt(ms + NORM_EPS) * g

    ua = _dot(xb, wab_ref[:, 0:3 * wa])
    arow_ref[...] = ua[:, wa:3 * wa]
    abf_ref[:, 0:wa] = (ua[:, 0:wa] * 0.125).astype(BF16)
    abf_ref[:, wa:3 * wa] = ua[:, wa:3 * wa].astype(BF16)

    ub = _dot(xb, wab_ref[:, 3 * wa:6 * wa])
    qb = headnorm(ub[:, 0:wa], gb_ref[0:1, :])
    kb = headnorm(ub[:, wa:2 * wa], gb_ref[1:2, :])
    vb = ub[:, 2 * wa:3 * wa]
    brow_ref[:, 0:wa] = kb
    brow_ref[:, wa:2 * wa] = vb
    bbf_ref[:, 0:wa] = (qb * 0.125).astype(BF16)
    bbf_ref[:, wa:2 * wa] = kb.astype(BF16)
    bbf_ref[:, 2 * wa:3 * wa] = vb.astype(BF16)

    n_chunk = 2 * N_HEADS

    def store_c_rows(first_chunk, x):
        for c in range(x.shape[1] // PAIR):
            crow_ref[pl.ds(first_chunk + c, tm, stride=n_chunk), :] = x[:, c * PAIR:(c + 1) * PAIR]

    uc = _dot(xb, wc_ref[...])
    for half in range(2):
        sl = slice(half * wa, (half + 1) * wa)
        qc = headnorm(uc[:, half * wa:(half + 1) * wa], gc_ref[0:1, sl])
        kc = headnorm(uc[:, wc + half * wa:wc + (half + 1) * wa], gc_ref[1:2, sl])
        cbf_ref[:, half * wa:(half + 1) * wa] = (qc * 0.125).astype(BF16)
        cbf_ref[:, wc + half * wa:wc + (half + 1) * wa] = kc.astype(BF16)
        store_c_rows(half * (wa // PAIR), kc)
    vc = uc[:, 2 * wc:3 * wc]
    store_c_rows(N_HEADS, vc)
    cbf_ref[:, 2 * wc:3 * wc] = vc.astype(BF16)

    lf = _log_sigmoid(_dot(xb, wf_ref[...]) + bf_ref[...])
    lf_ref[...] = lf

    @pl.when(i % tiles_per_seq == 0)
    def _():
        carry_ref[...] = jnp.zeros_like(carry_ref)

    hi, lo = _split(lf)
    tri = tri_ref[...]
    cs = _dot(tri, hi) + _dot(tri, lo) + carry_ref[...]
    cs_ref[...] = cs
    carry_ref[...] = cs[tm - 1:tm, :]


def _proj(h2d, tri, tiles_per_seq, layer, depth, prev_rows, g1, wab, wc, wf, bfp, gb, gc, seg):
    n, d = h2d.shape
    tm = tri.shape[0]
    wa = N_HEADS * HEAD_DIM
    n_chunk = 2 * N_HEADS
    const = lambda shape: pl.BlockSpec(shape, lambda i: (0,) * len(shape))
    row = lambda w: pl.BlockSpec((tm, w), lambda i: (i, 0))
    slab = lambda rows, w: pl.BlockSpec((None, rows, w), lambda i: (layer, i, 0))
    out_shapes = [
        jax.ShapeDtypeStruct((depth, n, 2 * wa), F32), jax.ShapeDtypeStruct((depth, n, 2 * wa), F32),
        jax.ShapeDtypeStruct((depth, n * n_chunk, PAIR), F32), jax.ShapeDtypeStruct((n, LANES), F32),
        jax.ShapeDtypeStruct((n, LANES), F32), jax.ShapeDtypeStruct((n, 3 * wa), BF16),
        jax.ShapeDtypeStruct((n, 3 * wa), BF16), jax.ShapeDtypeStruct((n, 6 * wa), BF16),
    ]
    inputs = [h2d, g1, wab, wc, wf, bfp, gb, gc, seg, tri]
    in_specs = [row(d), const((1, d)), const(wab.shape), const(wc.shape), const(wf.shape), const((1, LANES)),
                const(gb.shape), const(gc.shape), const(seg.shape), const(tri.shape)]
    aliases = {}
    if prev_rows is not None:
        aliases = {len(inputs) + k: k for k in range(len(prev_rows))}
        in_specs += [pl.BlockSpec(memory_space=pl.ANY)] * len(prev_rows)
        inputs += list(prev_rows)
    return pl.pallas_call(
        functools.partial(_proj_kernel, tiles_per_seq=tiles_per_seq),
        grid=(n // tm,),
        in_specs=in_specs,
        out_specs=[slab(tm, 2 * wa), slab(tm, 2 * wa), slab(tm * n_chunk, PAIR), row(LANES), row(LANES),
                   row(3 * wa), row(3 * wa), row(6 * wa)],
        out_shape=out_shapes,
        input_output_aliases=aliases,
        scratch_shapes=[pltpu.VMEM((1, LANES), F32)],
        compiler_params=_params(("arbitrary",)),
        name="proj",
    )(*inputs)


def _head_mask(hh):
    return (_iota((1, PAIR), 1) >> HEAD_SHIFT) == hh


def _store_head(o_ref, hh, hmask, res):
    @pl.when(hh == 0)
    def _():
        o_ref[...] = jnp.where(hmask, res, 0.0).astype(o_ref.dtype)

    @pl.when(hh != 0)
    def _():
        o_ref[...] = jnp.where(hmask, res.astype(o_ref.dtype), o_ref[...])


def _softmax_init(rows, width=PAIR):
    return (jnp.full((rows, 1), NEG, F32), jnp.zeros((rows, 1), F32), jnp.zeros((rows, width), F32))


def _softmax_probs(m, l, s):
    m_new = jnp.maximum(m, jnp.max(s, axis=1, keepdims=True))
    alpha = jnp.exp(m - m_new)
    p = jnp.exp(s - m_new)
    return m_new, alpha * l + jnp.sum(p, axis=1, keepdims=True), alpha, p.astype(BF16)


def _softmax_step(state, s, v):
    m, l, acc = state
    m, l, alpha, p = _softmax_probs(m, l, s)
    return m, l, alpha * acc + _dot(p, v)


def _stick_logs(z, valid=None):
    lb = _log_sigmoid(z)
    ls = lb - z
    if valid is not None:
        ls = jnp.where(valid, ls, 0.0)
    return lb, ls


def _stick_weights(lb, ls, later, run, valid=None):
    w = jnp.exp(lb + later + run)
    if valid is not None:
        w = jnp.where(valid, w, 0.0)
    return w, run + jnp.sum(ls, axis=1, keepdims=True)


def _stick_kernel(q_ref, k_ref, v_ref, ltri_ref, o_ref):
    i = pl.program_id(2)
    hh = pl.program_id(3)
    tq = q_ref.shape[0]
    hmask = _head_mask(hh)
    q = q_ref[...]
    qm = jnp.where(hmask, q, jnp.zeros_like(q))
    ltri = ltri_ref[...]
    valid = _iota((tq, tq), 1) < _iota((tq, tq), 0)

    def block(j, run, acc, mask):
        off = pl.multiple_of(j * tq, tq)
        lb, ls = _stick_logs(_dot_nt(qm, k_ref[pl.ds(off, tq), :]), mask)
        hi, lo = _split(ls)
        w, run = _stick_weights(lb, ls, _dot(hi, ltri) + _dot(lo, ltri), run, mask)
        return run, acc + _dot(w.astype(BF16), v_ref[pl.ds(off, tq), :])

    run, acc = block(i, jnp.zeros((tq, 1), F32), jnp.zeros((tq, PAIR), F32), valid)

    def cond(c):
        j, run, _ = c
        return jnp.logical_and(j >= 0, jnp.max(run) > -STICK_SKIP)

    def body(c):
        j, run, acc = c
        run, acc = block(j, run, acc, None)
        return j - 1, run, acc

    _, _, acc = lax.while_loop(cond, body, (i - 1, run, acc))
    _store_head(o_ref, hh, hmask, acc)


def _fox_kernel(reach_ref, q_ref, k_ref, v_ref, cq_ref, ck_ref, o_ref):
    pair = pl.program_id(1)
    i = pl.program_id(2)
    hh = pl.program_id(3)
    h = pair * 2 + hh
    tq = q_ref.shape[0]
    hmask = _head_mask(hh)
    q = q_ref[...]
    qm = jnp.where(hmask, q, jnp.zeros_like(q))
    cq = jnp.sum(jnp.where(_iota((1, LANES), 1) == h, cq_ref[...], 0.0), axis=1, keepdims=True)
    causal = _iota((tq, tq), 1) <= _iota((tq, tq), 0)

    def scores(j, diag):
        off = pl.multiple_of(j * tq, tq)
        ck = ck_ref[pl.ds(h, 1), pl.ds(off, tq)]
        s = _dot_nt(qm, k_ref[pl.ds(off, tq), :]) + (cq - ck)
        return jnp.where(causal, s, NEG) if diag else s

    def update(state, s, j):
        return _softmax_step(state, s, v_ref[pl.ds(pl.multiple_of(j * tq, tq), tq), :])

    def body(t, c):
        s, state = c
        j = i - 1 - t
        return scores(j, False), update(state, s, j + 1)

    n_old = reach_ref[pl.program_id(0), h, i]
    s, state = lax.fori_loop(0, n_old, body, (scores(i, True), _softmax_init(tq)))
    _, l, acc = update(state, s, i - n_old)
    _store_head(o_ref, hh, hmask, acc / l)


def _diff_kernel(slope_ref, lam_ref, reach_ref, q_ref, k_ref, v_ref, g_ref, o_ref, *, out_scale):
    h = pl.program_id(1)
    i = pl.program_id(2)
    tq = q_ref.shape[0]
    slope = slope_ref[h]
    lam = lam_ref[0]
    q = q_ref[...]
    qms = [jnp.where(_head_mask(c), q, jnp.zeros_like(q)) for c in range(2)]
    row = _iota((tq, tq), 0)
    col = _iota((tq, tq), 1)
    causal = col <= row
    sd0 = slope * (row - col).astype(F32)

    def scores(j, diag):
        k = k_ref[pl.ds(pl.multiple_of(j * tq, tq), tq), :]
        sb = sd0 + slope * ((i - j) * tq).astype(F32)
        out = []
        for c in range(2):
            s = _dot_nt(qms[c], k) - sb
            out.append(jnp.where(causal, s, NEG) if diag else s)
        return tuple(out)

    def update(states, ss, j):
        v = v_ref[pl.ds(pl.multiple_of(j * tq, tq), tq), :]
        return tuple(_softmax_step(states[c], ss[c], v) for c in range(2))

    def body(t, c):
        ss, states = c
        j = i - 1 - t
        return scores(j, False), update(states, ss, j + 1)

    n_old = jnp.minimum(i, reach_ref[h])
    ss, states = lax.fori_loop(0, n_old, body, (scores(i, True), (_softmax_init(tq), _softmax_init(tq))))
    (_, l0, a0), (_, l1, a1) = update(states, ss, i - n_old)
    o = a0 / l0 - lam * (a1 / l1)
    o_ref[...] = (_rms_rows(o, g_ref[...]) * out_scale).astype(o_ref.dtype)


def _qk_bound(gq, gk):
    return 1.02 * math.sqrt(HEAD_DIM) * jnp.max(jnp.abs(gq)) * jnp.max(jnp.abs(gk))


def _fox_reach(cs, bound, tq):
    c = cs[:, :, :N_HEADS]
    c_first = c[:, 0::tq, :]
    c_last = c[:, tq - 1::tq, :]
    nt = c_first.shape[1]
    i = jnp.arange(nt)[:, None]
    j = jnp.arange(nt)[None, :]
    bias = c_first[:, :, None, :] - c_last[:, None, :, :]
    need = jnp.logical_and(2.0 * bound + bias >= -SOFTMAX_SKIP, (j < i)[None, :, :, None])
    reach = jnp.max(jnp.where(need, (i - j)[None, :, :, None], 0), axis=2)
    return jnp.transpose(reach, (0, 2, 1)).astype(jnp.int32)


def _alibi_reach(bound, tq, n_tiles):
    dist = (2.0 * bound + SOFTMAX_SKIP) / jnp.asarray(ALIBI_SLOPES, F32)
    return jnp.clip(jnp.floor((dist - 1.0) / tq) + 1.0, 0, n_tiles).astype(jnp.int32)


def _prompt_mixers(abf, bbf, cbf, cs, ck, ltri, slopes, lam, gsub, out_scale, bound_b, bound_c):
    bsz, t, _ = abf.shape
    npair = N_HEADS // 2
    wa = N_HEADS * HEAD_DIM

    def pair_specs(tile):
        qs = pl.BlockSpec((None, tile, PAIR), lambda b, p, i, hh: (b, i, p))
        ks = pl.BlockSpec((None, t, PAIR), lambda b, p, i, hh: (b, 0, npair + p))
        vs = pl.BlockSpec((None, t, PAIR), lambda b, p, i, hh: (b, 0, 2 * npair + p))
        os = pl.BlockSpec((None, tile, PAIR), lambda b, p, i, hh: (b, i, p))
        return qs, ks, vs, os

    sem4 = ("arbitrary",) * 4
    qs, ks, vs, os = pair_specs(STICK_TILE)
    oa = pl.pallas_call(
        _stick_kernel,
        grid=(bsz, npair, t // STICK_TILE, 2),
        in_specs=[qs, ks, vs, pl.BlockSpec(ltri.shape, lambda b, p, i, hh: (0, 0))],
        out_specs=os,
        out_shape=jax.ShapeDtypeStruct((bsz, t, wa), BF16),
        compiler_params=_params(sem4),
        name="stick_prompt",
    )(abf, abf, abf, ltri)

    tq = SOFTMAX_TILE
    smem = pl.BlockSpec(memory_space=pltpu.SMEM)
    qs, ks, vs, os = pair_specs(tq)
    ob = pl.pallas_call(
        _fox_kernel,
        grid=(bsz, npair, t // tq, 2),
        in_specs=[smem, qs, ks, vs,
                  pl.BlockSpec((None, tq, LANES), lambda b, p, i, hh: (b, i, 0)),
                  pl.BlockSpec((None, 8, t), lambda b, p, i, hh: (b, 0, 0))],
        out_specs=os,
        out_shape=jax.ShapeDtypeStruct((bsz, t, wa), BF16),
        compiler_params=_params(sem4),
        name="fox_prompt",
    )(_fox_reach(cs, bound_b, tq), bbf, bbf, bbf, cs, ck)

    oc = pl.pallas_call(
        functools.partial(_diff_kernel, out_scale=out_scale),
        grid=(bsz, N_HEADS, t // tq),
        in_specs=[smem, smem, smem,
                  pl.BlockSpec((None, tq, PAIR), lambda b, h, i: (b, i, h)),
                  pl.BlockSpec((None, t, PAIR), lambda b, h, i: (b, 0, N_HEADS + h)),
                  pl.BlockSpec((None, t, PAIR), lambda b, h, i: (b, 0, 2 * N_HEADS + h)),
                  pl.BlockSpec((1, PAIR), lambda b, h, i: (0, 0))],
        out_specs=pl.BlockSpec((None, tq, PAIR), lambda b, h, i: (b, i, h)),
        out_shape=jax.ShapeDtypeStruct((bsz, t, 2 * wa), BF16),
        compiler_params=_params(("arbitrary",) * 3),
        name="diff_prompt",
    )(slopes, lam, _alibi_reach(bound_c, tq, t // tq), cbf, cbf, cbf, gsub)
    return oa, ob, oc


def _sample_kernel(pt_ref, lam_ref, qa_ref, qb_ref, qc_ref, csq_ref, csk_ref, ltri_ref, g_ref,
                   pool_a_ref, *rest, layer, pages_per_step, n_pages, out_scale):
    gsz = pages_per_step
    page_refs = rest[:3 * gsz]
    oa_ref, ob_ref, oc_ref = rest[3 * gsz:3 * gsz + 3]
    (ra_ref, acca_ref, mb_ref, lb_ref, accb_ref, carry_ref,
     mc_ref, lc_ref, accc_ref, abuf_ref, asem_ref) = rest[3 * gsz + 3:]
    j = pl.program_id(1)
    nq = qa_ref.shape[0]
    span = ltri_ref.shape[0]
    page = span // gsz
    wa = N_HEADS * HEAD_DIM
    wc = 2 * wa
    ra_rows = N_HEADS * nq
    rc_rows = 2 * ra_rows
    nq_shift = nq.bit_length() - 1
    lam = lam_ref[0]

    def stacked_q(x, reps):
        return jnp.concatenate([x] * reps, axis=0)

    def pad_rows(x):
        return jnp.concatenate([x, jnp.zeros((page - nq, x.shape[1]), x.dtype)], axis=0)

    def expand_heads(x):
        return jnp.concatenate(
            [jnp.broadcast_to(x[h:h + 1, :], (nq, x.shape[1])) for h in range(N_HEADS)], axis=0)

    qa = qa_ref[...]
    qb = qb_ref[...]
    qc = qc_ref[...]
    diag_a = (_iota((ra_rows, wa), 1) >> HEAD_SHIFT) == (_iota((ra_rows, wa), 0) >> nq_shift)
    diag_c = (_iota((rc_rows, wc), 1) >> HEAD_SHIFT) == (_iota((rc_rows, wc), 0) >> nq_shift)
    qma = jnp.where(diag_a, stacked_q(qa[:, 0:wa], N_HEADS), 0.0).astype(BF16)
    qmb = jnp.where(diag_a, stacked_q(qb[:, 0:wa], N_HEADS), 0.0).astype(BF16)
    qmc = jnp.where(diag_c, stacked_q(qc[:, 0:wc], 2 * N_HEADS), 0.0).astype(BF16)
    head_c = _iota((rc_rows, 1), 0) >> (nq_shift + 1)
    slope_c = jnp.full((rc_rows, 1), ALIBI_SLOPES[N_HEADS - 1], F32)
    for h in range(N_HEADS - 1):
        slope_c = jnp.where(head_c == h, ALIBI_SLOPES[h], slope_c)

    csq = csq_ref[...]
    lane8 = _iota((nq, LANES), 1)
    fnew = jnp.concatenate(
        [jnp.sum(jnp.where(lane8 == h, csq, 0.0), axis=1, keepdims=True) for h in range(N_HEADS)],
        axis=0)

    @pl.when(j == 0)
    def _():
        lane_k = _iota((ra_rows, page), 1)
        qi_a = _iota((ra_rows, page), 0) & (nq - 1)
        ltri = ltri_ref[0:page, 0:page]
        ka = pad_rows(qa[:, wa:2 * wa]).astype(BF16)
        va = pad_rows(qa[:, 2 * wa:3 * wa]).astype(BF16)
        lb, ls = _stick_logs(_dot_nt(qma, ka), lane_k < qi_a)
        hi, lo = _split(ls)
        w, run = _stick_weights(lb, ls, _dot(hi, ltri) + _dot(lo, ltri), jnp.zeros((ra_rows, 1), F32),
                                lane_k < qi_a)
        ra_ref[...] = run
        acca_ref[...] = _dot(w.astype(BF16), va)

        kb = pad_rows(qb[:, wa:2 * wa]).astype(BF16)
        vb = pad_rows(qb[:, 2 * wa:3 * wa]).astype(BF16)
        s = _dot_nt(qmb, kb) + (fnew - expand_heads(csk_ref[...]))
        m, l, acc = _softmax_step(_softmax_init(ra_rows, wa), jnp.where(lane_k <= qi_a, s, NEG), vb)
        mb_ref[...] = m
        lb_ref[...] = l
        accb_ref[...] = acc
        carry_ref[...] = jnp.zeros_like(carry_ref)

        lane_kc = _iota((rc_rows, page), 1)
        qi_c = _iota((rc_rows, page), 0) & (nq - 1)
        kc = pad_rows(qc[:, wc:2 * wc]).astype(BF16)
        vc = pad_rows(qc[:, 2 * wc:3 * wc]).astype(BF16)
        s = _dot_nt(qmc, kc) - slope_c * (qi_c - lane_kc).astype(F32)
        m, l, acc = _softmax_step(_softmax_init(rc_rows, wc), jnp.where(lane_kc <= qi_c, s, NEG), vc)
        mc_ref[...] = m
        lc_ref[...] = l
        accc_ref[...] = acc

    b_refs = page_refs[0::3]
    f_refs = page_refs[1::3]
    c_refs = page_refs[2::3]
    first_page = n_pages - (j + 1) * gsz
    ltri = ltri_ref[...]

    def lanes_cat(pages):
        return jnp.concatenate([p.astype(BF16) for p in pages], axis=1)

    def rows_cat(kv):
        return jnp.concatenate(
            [jnp.concatenate([r[pl.ds(kv * N_HEADS + h, page, stride=2 * N_HEADS), :].astype(BF16)
                              for h in range(N_HEADS)], axis=1) for r in c_refs], axis=0)

    @pl.when(jnp.max(ra_ref[...]) > -STICK_SKIP)
    def _():
        seq = pl.program_id(0)
        copies = [pltpu.make_async_copy(pool_a_ref.at[layer, pt_ref[seq, first_page + g]],
                                        abuf_ref.at[g], asem_ref.at[g]) for g in range(gsz)]
        for cp in copies:
            cp.start()
        for cp in copies:
            cp.wait()
        lb, ls = _stick_logs(_dot(qma, lanes_cat([abuf_ref[g, 0] for g in range(gsz)])))
        cum = _dot(jnp.concatenate(_split(ls), axis=0), ltri)
        w, run = _stick_weights(lb, ls, cum[0:ra_rows] + cum[ra_rows:2 * ra_rows], ra_ref[...])
        ra_ref[...] = run
        acca_ref[...] = acca_ref[...] + _dot_nt(w.astype(BF16),
                                                lanes_cat([abuf_ref[g, 1] for g in range(gsz)]))

    s_b = _dot(qmb, lanes_cat([r[0] for r in b_refs]))
    s_c = _dot_nt(qmc, rows_cat(0))

    lf_pages = [expand_heads(r[...]) for r in f_refs]
    cum = _dot(jnp.concatenate([part for lf in lf_pages for part in _split(lf)], axis=0),
               ltri[0:page, 0:page])
    later_pages = jnp.zeros((ra_rows, 1), F32)
    sfx_pages = [None] * gsz
    for g in reversed(range(gsz)):
        within = cum[2 * g * ra_rows:(2 * g + 1) * ra_rows] + cum[(2 * g + 1) * ra_rows:(2 * g + 2) * ra_rows]
        sfx_pages[g] = within + later_pages
        later_pages = later_pages + (within[:, 0:1] + lf_pages[g][:, 0:1])
    sfx_r = jnp.concatenate(sfx_pages, axis=1)
    carry = carry_ref[...]
    m_b, l_b, alpha_b, p_b = _softmax_probs(mb_ref[...], lb_ref[...], s_b + (sfx_r + (fnew + carry)))
    mb_ref[...] = m_b
    lb_ref[...] = l_b
    carry_ref[...] = carry + later_pages

    qi_c = _iota((rc_rows, span), 0) & (nq - 1)
    dist = (qi_c - _iota((rc_rows, span), 1) + (n_pages - first_page) * page).astype(F32)
    m_c, l_c, alpha_c, p_c = _softmax_probs(mc_ref[...], lc_ref[...], s_c - slope_c * dist)
    mc_ref[...] = m_c
    lc_ref[...] = l_c

    accb_ref[...] = alpha_b * accb_ref[...] + _dot_nt(p_b, lanes_cat([r[1] for r in b_refs]))
    accc_ref[...] = alpha_c * accc_ref[...] + _dot(p_c, rows_cat(1))

    @pl.when(j == pl.num_programs(1) - 1)
    def _():
        lane_o = _iota((nq, wa), 1) >> HEAD_SHIFT
        acc_a = acca_ref[...]
        nb = accb_ref[...] / lb_ref[...]
        oa = jnp.zeros((nq, wa), F32)
        ob = jnp.zeros((nq, wa), F32)
        for h in range(N_HEADS):
            oa = jnp.where(lane_o == h, acc_a[h * nq:(h + 1) * nq, :], oa)
            ob = jnp.where(lane_o == h, nb[h * nq:(h + 1) * nq, :], ob)
        oa_ref[...] = oa
        ob_ref[...] = ob
        nc = accc_ref[...] / lc_ref[...]
        for h in range(N_HEADS):
            blk = nc[2 * h * nq:2 * (h + 1) * nq, h * PAIR:(h + 1) * PAIR]
            o = blk[0:nq, :] - lam * blk[nq:2 * nq, :]
            oc_ref[:, h * PAIR:(h + 1) * PAIR] = _rms_rows(o, g_ref[...]) * out_scale


def _sample_mixers(layer, page_table, lam, abf, bbf, cbf, csq, csk, ltri, gsub,
                   pool_a, pool_b, pool_f, pool_c, out_scale):
    dbsz, nq, _ = abf.shape
    n_pages = page_table.shape[1]
    gsz = PAGES_PER_STEP
    page = ltri.shape[0] // gsz
    wa = N_HEADS * HEAD_DIM
    steps = n_pages // gsz

    def seq_spec(w):
        return pl.BlockSpec((None, nq, w), lambda b, j, pt: (b, 0, 0))

    def page_specs(g):
        def idx(b, j, pt):
            return pt[b, n_pages - (j + 1) * gsz + g]
        return [
            pl.BlockSpec((None, None, 2, wa, page), lambda b, j, pt: (layer, idx(b, j, pt), 0, 0, 0)),
            pl.BlockSpec((None, None, N_HEADS, page), lambda b, j, pt: (layer, idx(b, j, pt), 0, 0)),
            pl.BlockSpec((None, None, 2 * N_HEADS * page, PAIR),
                         lambda b, j, pt: (layer, idx(b, j, pt), 0, 0)),
        ]

    in_specs = [pl.BlockSpec(memory_space=pltpu.SMEM), seq_spec(3 * wa), seq_spec(3 * wa), seq_spec(6 * wa),
                seq_spec(LANES), pl.BlockSpec((None, 8, LANES), lambda b, j, pt: (b, 0, 0)),
                pl.BlockSpec(ltri.shape, lambda b, j, pt: (0, 0)),
                pl.BlockSpec((1, PAIR), lambda b, j, pt: (0, 0)),
                pl.BlockSpec(memory_space=pl.ANY)]
    pools = []
    for g in range(gsz):
        in_specs += page_specs(g)
        pools += [pool_b, pool_f, pool_c]
    ra_rows = N_HEADS * nq
    rc_rows = 2 * ra_rows
    scratch = [pltpu.VMEM((ra_rows, 1), F32), pltpu.VMEM((ra_rows, wa), F32),
               pltpu.VMEM((ra_rows, 1), F32), pltpu.VMEM((ra_rows, 1), F32), pltpu.VMEM((ra_rows, wa), F32),
               pltpu.VMEM((ra_rows, 1), F32),
               pltpu.VMEM((rc_rows, 1), F32), pltpu.VMEM((rc_rows, 1), F32), pltpu.VMEM((rc_rows, 2 * wa), F32),
               pltpu.VMEM((gsz, 2, wa, page), F32), pltpu.SemaphoreType.DMA((gsz,))]
    return pl.pallas_call(
        functools.partial(_sample_kernel, layer=layer, pages_per_step=gsz, n_pages=n_pages,
                          out_scale=out_scale),
        grid_spec=pltpu.PrefetchScalarGridSpec(
            num_scalar_prefetch=1,
            grid=(dbsz, steps),
            in_specs=in_specs,
            out_specs=[seq_spec(wa), seq_spec(wa), seq_spec(2 * wa)],
            scratch_shapes=scratch),
        out_shape=[jax.ShapeDtypeStruct((dbsz, nq, wa), F32), jax.ShapeDtypeStruct((dbsz, nq, wa), F32),
                   jax.ShapeDtypeStruct((dbsz, nq, 2 * wa), F32)],
        compiler_params=_params(("arbitrary", "arbitrary")),
        name="sample_mixers",
    )(page_table, lam, abf, bbf, cbf, csq, csk, ltri, gsub, pool_a, *pools)


def _merge_kernel(h_ref, oa_ref, ob_ref, oc_ref, g1_ref, wa_ref, wb_ref, wc_ref, wg_ref, bg_ref, wo_ref,
                  g2_ref, wrh_ref, wrl_ref, br_ref, h2_ref, x2_ref, gates_ref):
    x = h_ref[...]
    d = x.shape[1]
    xb = _rms_rows(x, g1_ref[...]).astype(BF16)
    ys = (_dot(oa_ref[...].astype(BF16), wa_ref[...]),
          _dot(ob_ref[...].astype(BF16), wb_ref[...]),
          _dot(oc_ref[...].astype(BF16), wc_ref[...]))
    merged = jnp.zeros_like(x)
    for idx, y in enumerate(ys):
        gl = _dot(xb, wg_ref[:, idx * d:(idx + 1) * d]) + bg_ref[:, idx * d:(idx + 1) * d]
        merged = merged + y / (1.0 + jnp.exp(-gl))
    h2 = x + _dot(merged.astype(BF16), wo_ref[...])
    h2_ref[...] = h2
    x2 = _rms_rows(h2, g2_ref[...])
    x2_ref[...] = x2.astype(BF16)

    xh, xl = _split(x2)
    wrh = wrh_ref[...]
    logits = _dot(xh, wrh) + _dot(xl, wrh) + _dot(xh, wrl_ref[...]) + br_ref[...]
    lane = _iota((1, LANES), 1)
    lane_f = lane.astype(F32)
    n_exp = N_GROUPS * EXPERTS_PER_GROUP
    is_exp = jnp.logical_and(lane >= GATE_LANE0, lane < GATE_LANE0 + n_exp)
    grp_of_lane = jnp.where(is_exp, ((lane - GATE_LANE0) >> GROUP_SHIFT).astype(F32), -1.0)

    def first_max(vals):
        mx = jnp.max(vals, axis=1, keepdims=True)
        idx = jnp.min(jnp.where(vals == mx, lane_f, float(LANES)), axis=1, keepdims=True)
        return mx, idx

    glog = jnp.where(lane < N_GROUPS, logits, NEG)
    gmax, gidx = first_max(glog)
    p_g = 1.0 / jnp.sum(jnp.exp(glog - gmax), axis=1, keepdims=True)
    elog = jnp.where(grp_of_lane == gidx, logits, NEG)
    m1, i1 = first_max(elog)
    elog2 = jnp.where(lane_f == i1, NEG, elog)
    m2, i2 = first_max(elog2)
    e2 = jnp.exp(m2 - m1)
    w1 = p_g / (1.0 + e2)
    w2 = p_g * e2 / (1.0 + e2)
    gates_ref[...] = jnp.where(lane_f == i1, w1, 0.0) + jnp.where(lane_f == i2, w2, 0.0)


def _merge(h2d, oa, ob, oc, g1, wa, wb, wc, wg, bg, wo, g2, wrh, wrl, br, tm):
    n, d = h2d.shape
    const = lambda a: pl.BlockSpec(a.shape, lambda i: (0,) * a.ndim)
    row = lambda w: pl.BlockSpec((tm, w), lambda i: (i, 0))
    return pl.pallas_call(
        _merge_kernel,
        grid=(n // tm,),
        in_specs=[row(d), row(oa.shape[1]), row(ob.shape[1]), row(oc.shape[1]), const(g1), const(wa),
                  const(wb), const(wc), const(wg), const(bg), const(wo), const(g2), const(wrh),
                  const(wrl), const(br)],
        out_specs=[row(d), row(d), row(LANES)],
        out_shape=[jax.ShapeDtypeStruct((n, d), F32), jax.ShapeDtypeStruct((n, d), BF16),
                   jax.ShapeDtypeStruct((n, LANES), F32)],
        compiler_params=_params(("arbitrary",)),
        name="merge",
    )(h2d, oa, ob, oc, g1, wa, wb, wc, wg, bg, wo, g2, wrh, wrl, br)


def _moe_kernel(x2_ref, gates_ref, h2_ref, wg_ref, wu_ref, wd_ref, o_ref):
    g = pl.program_id(1)
    x = x2_ref[...]
    hg = _dot(x, wg_ref[...])
    hu = _dot(x, wu_ref[...])
    act = hg / (1.0 + jnp.exp(-hg)) * hu
    gates = gates_ref[...]
    lane = _iota((1, LANES), 1)
    parts = []
    for e in range(EXPERTS_PER_GROUP):
        col = jnp.sum(jnp.where(lane == GATE_LANE0 + g * EXPERTS_PER_GROUP + e, gates, 0.0),
                      axis=1, keepdims=True)
        parts.append((act[:, e * D_EXPERT:(e + 1) * D_EXPERT] * col).astype(BF16))
    y = _dot(jnp.concatenate(parts, axis=1), wd_ref[...])

    @pl.when(g == 0)
    def _():
        o_ref[...] = h2_ref[...] + y

    @pl.when(g != 0)
    def _():
        o_ref[...] = o_ref[...] + y


def _moe(x2, gates, h2, wg, wu, wd, tm):
    n, d = h2.shape
    gw = EXPERTS_PER_GROUP * D_EXPERT
    return pl.pallas_call(
        _moe_kernel,
        grid=(n // tm, N_GROUPS),
        in_specs=[pl.BlockSpec((tm, d), lambda i, g: (i, 0)), pl.BlockSpec((tm, LANES), lambda i, g: (i, 0)),
                  pl.BlockSpec((tm, d), lambda i, g: (i, 0)), pl.BlockSpec((d, gw), lambda i, g: (0, g)),
                  pl.BlockSpec((d, gw), lambda i, g: (0, g)), pl.BlockSpec((gw, d), lambda i, g: (g, 0))],
        out_specs=pl.BlockSpec((tm, d), lambda i, g: (i, 0)),
        out_shape=jax.ShapeDtypeStruct((n, d), F32),
        compiler_params=_params(("arbitrary", "arbitrary")),
        name="moe",
    )(x2, gates, h2, wg, wu, wd)


def _pad_lanes(x, lane0=0):
    return jnp.pad(x, ((0, 0), (lane0, LANES - lane0 - x.shape[1])))


def _strict_lower(n):
    return jnp.tril(jnp.ones((n, n), F32), -1).astype(BF16)


def kernel(x_prompt, x_sample, cache_a_kv, cache_b_kv, cache_b_logf, cache_c_kv, page_table, norm1_g, w_in, b_forget, qn_b_g, kn_b_g, qn_c_g, kn_c_g, lam_q1, lam_k1, lam_q2, lam_k2, subln_c_g, w_br_a, w_br_b, w_br_c, w_gate, b_gate, w_out, norm2_g, w_router_grp, b_router_grp, w_router_exp, b_router_exp, w_exp_gate, w_exp_up, w_exp_down):
    bsz, seq, d = x_prompt.shape
    dbsz, nq, _ = x_sample.shape
    depth = w_in.shape[0]
    n_pool, page = cache_a_kv.shape[1], cache_a_kv.shape[2]
    wa = N_HEADS * HEAD_DIM
    n_exp = N_GROUPS * EXPERTS_PER_GROUP
    ns = dbsz * nq
    assert nq == 8 and d % LANES == 0 and seq % TOKEN_TILE == 0
    assert page_table.shape[1] % PAGES_PER_STEP == 0

    pool_a = jnp.transpose(cache_a_kv, (0, 1, 3, 4, 5, 2)).reshape(depth, n_pool, 2, wa, page)
    pool_b = jnp.transpose(cache_b_kv, (0, 1, 3, 4, 5, 2)).reshape(depth, n_pool, 2, wa, page)
    pool_f = jnp.transpose(cache_b_logf, (0, 1, 3, 2))
    pool_c = cache_c_kv.reshape(depth, n_pool, page * 2 * N_HEADS, PAIR)

    seg = (jnp.kron(jnp.eye(N_HEADS, dtype=F32), jnp.ones((HEAD_DIM, HEAD_DIM), F32)) / HEAD_DIM).astype(BF16)
    tri_p = jnp.tril(jnp.ones((TOKEN_TILE, TOKEN_TILE), F32)).astype(BF16)
    r = jnp.arange(ns)
    tri_s = jnp.logical_and(r[:, None] >= r[None, :], (r[:, None] // nq) == (r[None, :] // nq)).astype(BF16)
    ltri_a = _strict_lower(STICK_TILE)
    ltri_pg = _strict_lower(PAGES_PER_STEP * page)
    slopes = jnp.asarray(ALIBI_SLOPES, F32)

    hp = x_prompt.reshape(bsz * seq, d)
    hs = x_sample.reshape(ns, d)
    rows_p = rows_s = None
    lf_p = []
    lf_s = []

    for l in range(depth):
        lam_init = 0.8 - 0.6 * math.exp(-0.3 * l)
        lam = (jnp.exp(jnp.sum(lam_q1[l].astype(F32) * lam_k1[l].astype(F32)))
               - jnp.exp(jnp.sum(lam_q2[l].astype(F32) * lam_k2[l].astype(F32))) + lam_init).reshape(1)
        out_scale = 1.0 - lam_init

        w = w_in[l]
        off_f = 6 * wa
        wab = w[:, :off_f].astype(BF16)
        wcc = w[:, off_f + N_HEADS:].astype(BF16)
        wf =_pad_lanes(w[:, off_f:off_f + N_HEADS]).astype(BF16)
        bfp = _pad_lanes(b_forget[l].reshape(1, N_HEADS).astype(F32))
        g1 = norm1_g[l].reshape(1, d)
        g2 = norm2_g[l].reshape(1, d)
        gb = jnp.stack([jnp.tile(qn_b_g[l], N_HEADS), jnp.tile(kn_b_g[l], N_HEADS)])
        gc = jnp.stack([jnp.tile(qn_c_g[l], 2 * N_HEADS), jnp.tile(kn_c_g[l], 2 * N_HEADS)])
        gsub = subln_c_g[l].reshape(1, PAIR)
        wr = jnp.concatenate([_pad_lanes(w_router_grp[l])[:, :GATE_LANE0],
                              _pad_lanes(w_router_exp[l], GATE_LANE0)[:, GATE_LANE0:]], axis=1)
        wrh = wr.astype(BF16)
        wrl = (wr - wrh.astype(F32)).astype(BF16)
        br = (_pad_lanes(b_router_grp[l].reshape(1, N_GROUPS))
              + _pad_lanes(b_router_exp[l].reshape(1, n_exp), GATE_LANE0))
        weg = jnp.transpose(w_exp_gate[l], (1, 0, 2)).reshape(d, n_exp * D_EXPERT).astype(BF16)
        weu = jnp.transpose(w_exp_up[l], (1, 0, 2)).reshape(d, n_exp * D_EXPERT).astype(BF16)
        wed = w_exp_down[l].reshape(n_exp * D_EXPERT, d).astype(BF16)
        merge_w = (g1, w_br_a[l].astype(BF16), w_br_b[l].astype(BF16), w_br_c[l].astype(BF16),
                   w_gate[l].astype(BF16), b_gate[l].reshape(1, -1), w_out[l].astype(BF16), g2, wrh, wrl, br)
        proj_w = (g1, wab, wcc, wf, bfp, gb, gc, seg)

        *rows_p, lf, cs, abf, bbf, cbf = _proj(hp, tri_p, seq // TOKEN_TILE, l, depth, rows_p, *proj_w)
        lf_p.append(lf[:, :N_HEADS].reshape(bsz, seq, N_HEADS))
        ck = jnp.transpose(cs.reshape(bsz, seq, LANES)[:, :, :8], (0, 2, 1))
        oa, ob, oc = _prompt_mixers(abf.reshape(bsz, seq, -1), bbf.reshape(bsz, seq, -1),
                                    cbf.reshape(bsz, seq, -1), cs.reshape(bsz, seq, LANES), ck,
                                    ltri_a, slopes, lam, gsub, out_scale,
                                    _qk_bound(qn_b_g[l], kn_b_g[l]), _qk_bound(qn_c_g[l], kn_c_g[l]))
        h2, x2, gates = _merge(hp, oa.reshape(bsz * seq, -1), ob.reshape(bsz * seq, -1),
                               oc.reshape(bsz * seq, -1), *merge_w, TOKEN_TILE)
        hp = _moe(x2, gates, h2, weg, weu, wed, TOKEN_TILE)

        *rows_s, lf, cs, abf, bbf, cbf = _proj(hs, tri_s, 1, l, depth, rows_s, *proj_w)
        lf_s.append(lf[:, :N_HEADS].reshape(dbsz, nq, N_HEADS))
        csq = cs.reshape(dbsz, nq, LANES)
        csk = jnp.pad(jnp.transpose(csq[:, :, :8], (0, 2, 1)), ((0, 0), (0, 0), (0, LANES - nq)))
        oa, ob, oc = _sample_mixers(l, page_table, lam, abf.astype(F32).reshape(dbsz, nq, -1),
                                    bbf.astype(F32).reshape(dbsz, nq, -1),
                                    cbf.astype(F32).reshape(dbsz, nq, -1), csq, csk, ltri_pg, gsub,
                                    pool_a, pool_b, pool_f, pool_c, out_scale)
        h2, x2, gates = _merge(hs, oa.reshape(ns, -1), ob.reshape(ns, -1), oc.reshape(ns, -1), *merge_w, ns)
        hs = _moe(x2, gates, h2, weg, weu, wed, ns)

    def cache_rows(rows, lfs, nb, nt):
        a, b, c = rows
        return (a.reshape(depth, nb, nt, 2, N_HEADS, HEAD_DIM), b.reshape(depth, nb, nt, 2, N_HEADS, HEAD_DIM),
                jnp.stack(lfs), c.reshape(depth, nb, nt, 2, N_HEADS, 2 * HEAD_DIM))

    return (hp.reshape(bsz, seq, d), hs.reshape(dbsz, nq, d),
            *cache_rows(rows_p, lf_p, bsz, seq), *cache_rows(rows_s, lf_s, dbsz, nq))
```

```python
import functools
import math

import jax
import jax.numpy as jnp
from jax import lax
from jax.experimental import pallas as pl
from jax.experimental.pallas import tpu as pltpu

F32 = jnp.float32
BF16 = jnp.bfloat16

HEAD_DIM = 64
HEAD_SHIFT = 6
N_HEADS = 4
PAIR = 2 * HEAD_DIM
N_GROUPS = 4
EXPERTS_PER_GROUP = 8
GROUP_SHIFT = 3
D_EXPERT = 128
NORM_EPS = 1e-6
NEG = -1e30
LANES = 128
GATE_LANE0 = 8
STICK_SKIP = 110.0
SOFTMAX_SKIP = 110.0
ALIBI_SLOPES = tuple(2.0 ** (-8.0 * (h + 1) / N_HEADS) for h in range(N_HEADS))
VMEM_LIMIT = 56 * 1024 * 1024

TOKEN_TILE = 512
STICK_TILE = 256
FOX_TILE = 512
DIFF_TILE = 512
PAGES_PER_STEP = 8


def _dot(a, b):
    return jnp.dot(a, b, preferred_element_type=F32)


def _dot_nt(a, b):
    return lax.dot_general(a, b, (((1,), (1,)), ((), ())), preferred_element_type=F32)


def _split(x):
    hi = x.astype(BF16)
    lo = (x - hi.astype(F32)).astype(BF16)
    return hi, lo


def _log_sigmoid(x):
    return jnp.minimum(x, 0.0) - jnp.log1p(jnp.exp(-jnp.abs(x)))


def _rms_rows(x, g):
    return x * lax.rsqrt(jnp.mean(x * x, axis=-1, keepdims=True) + NORM_EPS) * g


def _params(sem):
    return pltpu.CompilerParams(dimension_semantics=sem, vmem_limit_bytes=VMEM_LIMIT)


def _iota(shape, dim):
    return lax.broadcasted_iota(jnp.int32, shape, dim)


def _proj_kernel(h_ref, g1_ref, wab_ref, wc_ref, wf_ref, bf_ref, gb_ref, gc_ref, seg_ref, tri_ref, *rest,
                 tiles_per_seq):
    (arow_ref, brow_ref, crow_ref, lf_ref, cs_ref, abf_ref, bbf_ref, cbf_ref, carry_ref) = rest[-9:]
    i = pl.program_id(0)
    tm = h_ref.shape[0]
    xb = _rms_rows(h_ref[...], g1_ref[...]).astype(BF16)
    seg = seg_ref[...]
    wa = N_HEADS * HEAD_DIM
    wc = 2 * wa

    def headnorm(u, g):
        hi, lo = _split(u * u)
        ms = _dot(hi, seg) + _dot(lo, seg)
        return u * lax.rsqrt(ms + NORM_EPS) * g

    ua = _dot(xb, wab_ref[:, 0:3 * wa])
    arow_ref[...] = ua[:, wa:3 * wa]
    abf_ref[:, 0:wa] = (ua[:, 0:wa] * 0.125).astype(BF16)
    abf_ref[:, wa:3 * wa] = ua[:, wa:3 * wa].astype(BF16)

    ub = _dot(xb, wab_ref[:, 3 * wa:6 * wa])
    qb = headnorm(ub[:, 0:wa], gb_ref[0:1, :])
    kb = headnorm(ub[:, wa:2 * wa], gb_ref[1:2, :])
    vb = ub[:, 2 * wa:3 * wa]
    brow_ref[:, 0:wa] = kb
    brow_ref[:, wa:2 * wa] = vb
    bbf_ref[:, 0:wa] = (qb * 0.125).astype(BF16)
    bbf_ref[:, wa:2 * wa] = kb.astype(BF16)
    bbf_ref[:, 2 * wa:3 * wa] = vb.astype(BF16)

    n_chunk = 2 * N_HEADS

    def store_c_rows(first_chunk, x):
        for c in range(x.shape[1] // PAIR):
            crow_ref[pl.ds(first_chunk + c, tm, stride=n_chunk), :] = x[:, c * PAIR:(c + 1) * PAIR]

    uc = _dot(xb, wc_ref[...])
    for half in range(2):
        sl = slice(half * wa, (half + 1) * wa)
        qc = headnorm(uc[:, half * wa:(half + 1) * wa], gc_ref[0:1, sl])
        kc = headnorm(uc[:, wc + half * wa:wc + (half + 1) * wa], gc_ref[1:2, sl])
        cbf_ref[:, half * wa:(half + 1) * wa] = (qc * 0.125).astype(BF16)
        cbf_ref[:, wc + half * wa:wc + (half + 1) * wa] = kc.astype(BF16)
        store_c_rows(half * (wa // PAIR), kc)
    vc = uc[:, 2 * wc:3 * wc]
    store_c_rows(N_HEADS, vc)
    cbf_ref[:, 2 * wc:3 * wc] = vc.astype(BF16)

    lf = _log_sigmoid(_dot(xb, wf_ref[...]) + bf_ref[...])
    lf_ref[...] = lf

    @pl.when(i % tiles_per_seq == 0)
    def _():
        carry_ref[...] = jnp.zeros_like(carry_ref)

    hi, lo = _split(lf)
    tri = tri_ref[...]
    cs = _dot(tri, hi) + _dot(tri, lo) + carry_ref[...]
    cs_ref[...] = cs
    carry_ref[...] = cs[tm - 1:tm, :]


def _proj(h2d, tri, tiles_per_seq, layer, depth, prev_rows, g1, wab, wc, wf, bfp, gb, gc, seg):
    n, d = h2d.shape
    tm = tri.shape[0]
    wa = N_HEADS * HEAD_DIM
    n_chunk = 2 * N_HEADS
    const = lambda shape: pl.BlockSpec(shape, lambda i: (0,) * len(shape))
    row = lambda w: pl.BlockSpec((tm, w), lambda i: (i, 0))
    slab = lambda rows, w: pl.BlockSpec((None, rows, w), lambda i: (layer, i, 0))
    out_shapes = [
        jax.ShapeDtypeStruct((depth, n, 2 * wa), F32), jax.ShapeDtypeStruct((depth, n, 2 * wa), F32),
        jax.ShapeDtypeStruct((depth, n * n_chunk, PAIR), F32), jax.ShapeDtypeStruct((n, LANES), F32),
        jax.ShapeDtypeStruct((n, LANES), F32), jax.ShapeDtypeStruct((n, 3 * wa), BF16),
        jax.ShapeDtypeStruct((n, 3 * wa), BF16), jax.ShapeDtypeStruct((n, 6 * wa), BF16),
    ]
    inputs = [h2d, g1, wab, wc, wf, bfp, gb, gc, seg, tri]
    in_specs = [row(d), const((1, d)), const(wab.shape), const(wc.shape), const(wf.shape), const((1, LANES)),
                const(gb.shape), const(gc.shape), const(seg.shape), const(tri.shape)]
    aliases = {}
    if prev_rows is not None:
        aliases = {len(inputs) + k: k for k in range(len(prev_rows))}
        in_specs += [pl.BlockSpec(memory_space=pl.ANY)] * len(prev_rows)
        inputs += list(prev_rows)
    return pl.pallas_call(
        functools.partial(_proj_kernel, tiles_per_seq=tiles_per_seq),
        grid=(n // tm,),
        in_specs=in_specs,
        out_specs=[slab(tm, 2 * wa), slab(tm, 2 * wa), slab(tm * n_chunk, PAIR), row(LANES), row(LANES),
                   row(3 * wa), row(3 * wa), row(6 * wa)],
        out_shape=out_shapes,
        input_output_aliases=aliases,
        scratch_shapes=[pltpu.VMEM((1, LANES), F32)],
        compiler_params=_params(("arbitrary",)),
        name="proj",
    )(*inputs)


def _head_mask(hh):
    return (_iota((1, PAIR), 1) >> HEAD_SHIFT) == hh


def _store_head(o_ref, hh, hmask, res):
    @pl.when(hh == 0)
    def _():
        o_ref[...] = jnp.where(hmask, res, 0.0).astype(o_ref.dtype)

    @pl.when(hh != 0)
    def _():
        o_ref[...] = jnp.where(hmask, res.astype(o_ref.dtype), o_ref[...])


def _softmax_init(rows, width=PAIR):
    return (jnp.full((rows, 1), NEG, F32), jnp.zeros((rows, 1), F32), jnp.zeros((rows, width), F32))


def _softmax_probs(m, l, s):
    m_new = jnp.maximum(m, jnp.max(s, axis=1, keepdims=True))
    alpha = jnp.exp(m - m_new)
    p = jnp.exp(s - m_new)
    return m_new, alpha * l + jnp.sum(p, axis=1, keepdims=True), alpha, p.astype(BF16)


def _softmax_step(state, s, v):
    m, l, acc = state
    m, l, alpha, p = _softmax_probs(m, l, s)
    return m, l, alpha * acc + _dot(p, v)


def _stick_logs(z, valid=None):
    lb = _log_sigmoid(z)
    ls = lb - z
    if valid is not None:
        ls = jnp.where(valid, ls, 0.0)
    return lb, ls


def _stick_weights(lb, ls, later, run, valid=None):
    w = jnp.exp(lb + later + run)
    if valid is not None:
        w = jnp.where(valid, w, 0.0)
    return w, run + jnp.sum(ls, axis=1, keepdims=True)


def _stick_kernel(q_ref, k_ref, v_ref, ltri_ref, o_ref):
    i = pl.program_id(2)
    hh = pl.program_id(3)
    tq = q_ref.shape[0]
    hmask = _head_mask(hh)
    q = q_ref[...]
    qm = jnp.where(hmask, q, jnp.zeros_like(q))
    ltri = ltri_ref[...]
    valid = _iota((tq, tq), 1) < _iota((tq, tq), 0)

    def block(j, run, acc, mask):
        off = pl.multiple_of(j * tq, tq)
        lb, ls = _stick_logs(_dot_nt(qm, k_ref[pl.ds(off, tq), :]), mask)
        hi, lo = _split(ls)
        w, run = _stick_weights(lb, ls, _dot(hi, ltri) + _dot(lo, ltri), run, mask)
        return run, acc + _dot(w.astype(BF16), v_ref[pl.ds(off, tq), :])

    run, acc = block(i, jnp.zeros((tq, 1), F32), jnp.zeros((tq, PAIR), F32), valid)

    def cond(c):
        j, run, _ = c
        return jnp.logical_and(j >= 0, jnp.max(run) > -STICK_SKIP)

    def body(c):
        j, run, acc = c
        run, acc = block(j, run, acc, None)
        return j - 1, run, acc

    _, _, acc = lax.while_loop(cond, body, (i - 1, run, acc))
    _store_head(o_ref, hh, hmask, acc)


def _fox_kernel(reach_ref, q_ref, k_ref, v_ref, ck_ref, o_ref):
    pair = pl.program_id(1)
    i = pl.program_id(2)
    hh = pl.program_id(3)
    h = pair * 2 + hh
    tq = q_ref.shape[0]
    hmask = _head_mask(hh)
    q = q_ref[...]
    qm = jnp.where(hmask, q, jnp.zeros_like(q))
    causal = _iota((tq, tq), 1) <= _iota((tq, tq), 0)
    c_tile = ck_ref[pl.ds(h, 1), pl.ds(pl.multiple_of(i * tq, tq), tq)][:, 0:1]

    def scores(j, diag):
        off = pl.multiple_of(j * tq, tq)
        ck = ck_ref[pl.ds(h, 1), pl.ds(off, tq)]
        s = _dot_nt(qm, k_ref[pl.ds(off, tq), :]) + (c_tile - ck)
        return jnp.where(causal, s, NEG) if diag else s

    def update(state, s, j):
        return _softmax_step(state, s, v_ref[pl.ds(pl.multiple_of(j * tq, tq), tq), :])

    def body(t, c):
        s, state = c
        j = i - 1 - t
        return scores(j, False), update(state, s, j + 1)

    n_old = reach_ref[pl.program_id(0), h, i]
    s, state = lax.fori_loop(0, n_old, body, (scores(i, True), _softmax_init(tq)))
    _, l, acc = update(state, s, i - n_old)
    _store_head(o_ref, hh, hmask, acc / l)


def _diff_kernel(slope_ref, lam_ref, reach_ref, q_ref, k_ref, v_ref, g_ref, o_ref, *, out_scale):
    h = pl.program_id(1)
    i = pl.program_id(2)
    tq = q_ref.shape[0]
    slope = slope_ref[h]
    lam = lam_ref[0]
    q = q_ref[...]
    qms = [jnp.where(_head_mask(c), q, jnp.zeros_like(q)) for c in range(2)]
    causal = _iota((tq, tq), 1) <= _iota((tq, tq), 0)
    col_bias = slope * _iota((1, tq), 1).astype(F32)

    def scores(j, diag):
        k = k_ref[pl.ds(pl.multiple_of(j * tq, tq), tq), :]
        bias = col_bias - slope * ((i - j) * tq).astype(F32)
        out = []
        for c in range(2):
            s = _dot_nt(qms[c], k) + bias
            out.append(jnp.where(causal, s, NEG) if diag else s)
        return tuple(out)

    def update(states, ss, j):
        v = v_ref[pl.ds(pl.multiple_of(j * tq, tq), tq), :]
        probs = [_softmax_probs(states[c][0], states[c][1], ss[c]) for c in range(2)]
        pv = _dot(jnp.concatenate([probs[0][3], probs[1][3]], axis=0), v)
        return tuple((probs[c][0], probs[c][1], probs[c][2] * states[c][2] + pv[c * tq:(c + 1) * tq])
                     for c in range(2))

    def body(t, c):
        ss, states = c
        j = i - 1 - t
        return scores(j, False), update(states, ss, j + 1)

    n_old = jnp.minimum(i, reach_ref[h])
    ss, states = lax.fori_loop(0, n_old, body, (scores(i, True), (_softmax_init(tq), _softmax_init(tq))))
    (_, l0, a0), (_, l1, a1) = update(states, ss, i - n_old)
    o = a0 / l0 - lam * (a1 / l1)
    o_ref[...] = (_rms_rows(o, g_ref[...]) * out_scale).astype(o_ref.dtype)


def _qk_bound(gq, gk):
    return 1.02 * math.sqrt(HEAD_DIM) * jnp.max(jnp.abs(gq)) * jnp.max(jnp.abs(gk))


def _fox_reach(cs, bound, tq):
    c = cs[:, :, :N_HEADS]
    c_first = c[:, 0::tq, :]
    c_last = c[:, tq - 1::tq, :]
    nt = c_first.shape[1]
    i = jnp.arange(nt)[:, None]
    j = jnp.arange(nt)[None, :]
    bias = c_first[:, :, None, :] - c_last[:, None, :, :]
    need = jnp.logical_and(2.0 * bound + bias >= -SOFTMAX_SKIP, (j < i)[None, :, :, None])
    reach = jnp.max(jnp.where(need, (i - j)[None, :, :, None], 0), axis=2)
    return jnp.transpose(reach, (0, 2, 1)).astype(jnp.int32)


def _alibi_reach(bound, tq, n_tiles):
    dist = (2.0 * bound + SOFTMAX_SKIP) / jnp.asarray(ALIBI_SLOPES, F32)
    return jnp.clip(jnp.floor((dist - 1.0) / tq) + 1.0, 0, n_tiles).astype(jnp.int32)


def _prompt_mixers(abf, bbf, cbf, cs, ck, ltri, slopes, lam, gsub, out_scale, bound_b, bound_c):
    bsz, t, _ = abf.shape
    npair = N_HEADS // 2
    wa = N_HEADS * HEAD_DIM

    def pair_specs(tile):
        qs = pl.BlockSpec((None, tile, PAIR), lambda b, p, i, hh: (b, i, p))
        ks = pl.BlockSpec((None, t, PAIR), lambda b, p, i, hh: (b, 0, npair + p))
        vs = pl.BlockSpec((None, t, PAIR), lambda b, p, i, hh: (b, 0, 2 * npair + p))
        os = pl.BlockSpec((None, tile, PAIR), lambda b, p, i, hh: (b, i, p))
        return qs, ks, vs, os

    sem4 = ("arbitrary",) * 4
    qs, ks, vs, os = pair_specs(STICK_TILE)
    oa = pl.pallas_call(
        _stick_kernel,
        grid=(bsz, npair, t // STICK_TILE, 2),
        in_specs=[qs, ks, vs, pl.BlockSpec(ltri.shape, lambda b, p, i, hh: (0, 0))],
        out_specs=os,
        out_shape=jax.ShapeDtypeStruct((bsz, t, wa), BF16),
        compiler_params=_params(sem4),
        name="stick_prompt",
    )(abf, abf, abf, ltri)

    tq = FOX_TILE
    smem = pl.BlockSpec(memory_space=pltpu.SMEM)
    qs, ks, vs, os = pair_specs(tq)
    ob = pl.pallas_call(
        _fox_kernel,
        grid=(bsz, npair, t // tq, 2),
        in_specs=[smem, qs, ks, vs, pl.BlockSpec((None, 8, t), lambda b, p, i, hh: (b, 0, 0))],
        out_specs=os,
        out_shape=jax.ShapeDtypeStruct((bsz, t, wa), BF16),
        compiler_params=_params(sem4),
        name="fox_prompt",
    )(_fox_reach(cs, bound_b, tq), bbf, bbf, bbf, ck)

    tq = DIFF_TILE
    oc = pl.pallas_call(
        functools.partial(_diff_kernel, out_scale=out_scale),
        grid=(bsz, N_HEADS, t // tq),
        in_specs=[smem, smem, smem,
                  pl.BlockSpec((None, tq, PAIR), lambda b, h, i: (b, i, h)),
                  pl.BlockSpec((None, t, PAIR), lambda b, h, i: (b, 0, N_HEADS + h)),
                  pl.BlockSpec((None, t, PAIR), lambda b, h, i: (b, 0, 2 * N_HEADS + h)),
                  pl.BlockSpec((1, PAIR), lambda b, h, i: (0, 0))],
        out_specs=pl.BlockSpec((None, tq, PAIR), lambda b, h, i: (b, i, h)),
        out_shape=jax.ShapeDtypeStruct((bsz, t, 2 * wa), BF16),
        compiler_params=_params(("arbitrary",) * 3),
        name="diff_prompt",
    )(slopes, lam, _alibi_reach(bound_c, tq, t // tq), cbf, cbf, cbf, gsub)
    return oa, ob, oc


def _sample_kernel(pt_ref, lam_ref, qa_ref, qb_ref, qc_ref, csq_ref, csk_ref, ltri_ref, g_ref,
                   pool_a_ref, *rest, layer, pages_per_step, n_pages, out_scale):
    gsz = pages_per_step
    page_refs = rest[:3 * gsz]
    oa_ref, ob_ref, oc_ref = rest[3 * gsz:3 * gsz + 3]
    (ra_ref, acca_ref, mb_ref, lb_ref, accb_ref, carry_ref,
     mc_ref, lc_ref, accc_ref, abuf_ref, asem_ref) = rest[3 * gsz + 3:]
    j = pl.program_id(1)
    nq = qa_ref.shape[0]
    span = ltri_ref.shape[0]
    page = span // gsz
    wa = N_HEADS * HEAD_DIM
    wc = 2 * wa
    ra_rows = N_HEADS * nq
    rc_rows = 2 * ra_rows
    nq_shift = nq.bit_length() - 1
    lam = lam_ref[0]

    def stacked_q(x, reps):
        return jnp.concatenate([x] * reps, axis=0)

    def pad_rows(x):
        return jnp.concatenate([x, jnp.zeros((page - nq, x.shape[1]), x.dtype)], axis=0)

    def expand_heads(x):
        return jnp.concatenate(
            [jnp.broadcast_to(x[h:h + 1, :], (nq, x.shape[1])) for h in range(N_HEADS)], axis=0)

    qa = qa_ref[...]
    qb = qb_ref[...]
    qc = qc_ref[...]
    diag_a = (_iota((ra_rows, wa), 1) >> HEAD_SHIFT) == (_iota((ra_rows, wa), 0) >> nq_shift)
    diag_c = (_iota((rc_rows, wc), 1) >> HEAD_SHIFT) == (_iota((rc_rows, wc), 0) >> nq_shift)
    qma = jnp.where(diag_a, stacked_q(qa[:, 0:wa], N_HEADS), 0.0).astype(BF16)
    qmb = jnp.where(diag_a, stacked_q(qb[:, 0:wa], N_HEADS), 0.0).astype(BF16)
    qmc = jnp.where(diag_c, stacked_q(qc[:, 0:wc], 2 * N_HEADS), 0.0).astype(BF16)
    head_c = _iota((rc_rows, 1), 0) >> (nq_shift + 1)
    slope_c = jnp.full((rc_rows, 1), ALIBI_SLOPES[N_HEADS - 1], F32)
    for h in range(N_HEADS - 1):
        slope_c = jnp.where(head_c == h, ALIBI_SLOPES[h], slope_c)

    csq = csq_ref[...]
    lane8 = _iota((nq, LANES), 1)
    fnew = jnp.concatenate(
        [jnp.sum(jnp.where(lane8 == h, csq, 0.0), axis=1, keepdims=True) for h in range(N_HEADS)],
        axis=0)

    @pl.when(j == 0)
    def _():
        lane_k = _iota((ra_rows, page), 1)
        qi_a = _iota((ra_rows, page), 0) & (nq - 1)
        ltri = ltri_ref[0:page, 0:page]
        ka = pad_rows(qa[:, wa:2 * wa]).astype(BF16)
        va = pad_rows(qa[:, 2 * wa:3 * wa]).astype(BF16)
        lb, ls = _stick_logs(_dot_nt(qma, ka), lane_k < qi_a)
        hi, lo = _split(ls)
        w, run = _stick_weights(lb, ls, _dot(hi, ltri) + _dot(lo, ltri), jnp.zeros((ra_rows, 1), F32),
                                lane_k < qi_a)
        ra_ref[...] = run
        acca_ref[...] = _dot(w.astype(BF16), va)

        kb = pad_rows(qb[:, wa:2 * wa]).astype(BF16)
        vb = pad_rows(qb[:, 2 * wa:3 * wa]).astype(BF16)
        s = _dot_nt(qmb, kb) + (fnew - expand_heads(csk_ref[...]))
        m, l, acc = _softmax_step(_softmax_init(ra_rows, wa), jnp.where(lane_k <= qi_a, s, NEG), vb)
        mb_ref[...] = m
        lb_ref[...] = l
        accb_ref[...] = acc
        carry_ref[...] = jnp.zeros_like(carry_ref)

        lane_kc = _iota((rc_rows, page), 1)
        qi_c = _iota((rc_rows, page), 0) & (nq - 1)
        kc = pad_rows(qc[:, wc:2 * wc]).astype(BF16)
        vc = pad_rows(qc[:, 2 * wc:3 * wc]).astype(BF16)
        s = _dot_nt(qmc, kc) - slope_c * (qi_c - lane_kc).astype(F32)
        m, l, acc = _softmax_step(_softmax_init(rc_rows, wc), jnp.where(lane_kc <= qi_c, s, NEG), vc)
        mc_ref[...] = m
        lc_ref[...] = l
        accc_ref[...] = acc

    b_refs = page_refs[0::3]
    f_refs = page_refs[1::3]
    c_refs = page_refs[2::3]
    first_page = n_pages - (j + 1) * gsz
    ltri = ltri_ref[...]

    def lanes_cat(pages):
        return jnp.concatenate([p.astype(BF16) for p in pages], axis=1)

    def rows_cat(kv):
        return jnp.concatenate(
            [jnp.concatenate([r[pl.ds(kv * N_HEADS + h, page, stride=2 * N_HEADS), :].astype(BF16)
                              for h in range(N_HEADS)], axis=1) for r in c_refs], axis=0)

    @pl.when(jnp.max(ra_ref[...]) > -STICK_SKIP)
    def _():
        seq = pl.program_id(0)
        copies = [pltpu.make_async_copy(pool_a_ref.at[layer, pt_ref[seq, first_page + g]],
                                        abuf_ref.at[g], asem_ref.at[g]) for g in range(gsz)]
        for cp in copies:
            cp.start()
        for cp in copies:
            cp.wait()
        lb, ls = _stick_logs(_dot(qma, lanes_cat([abuf_ref[g, 0] for g in range(gsz)])))
        cum = _dot(jnp.concatenate(_split(ls), axis=0), ltri)
        w, run = _stick_weights(lb, ls, cum[0:ra_rows] + cum[ra_rows:2 * ra_rows], ra_ref[...])
        ra_ref[...] = run
        acca_ref[...] = acca_ref[...] + _dot_nt(w.astype(BF16),
                                                lanes_cat([abuf_ref[g, 1] for g in range(gsz)]))

    s_b = _dot(qmb, lanes_cat([r[0] for r in b_refs]))
    s_c = _dot_nt(qmc, rows_cat(0))

    lf_pages = [expand_heads(r[...]) for r in f_refs]
    cum = _dot(jnp.concatenate([part for lf in lf_pages for part in _split(lf)], axis=0),
               ltri[0:page, 0:page])
    later_pages = jnp.zeros((ra_rows, 1), F32)
    sfx_pages = [None] * gsz
    for g in reversed(range(gsz)):
        within = cum[2 * g * ra_rows:(2 * g + 1) * ra_rows] + cum[(2 * g + 1) * ra_rows:(2 * g + 2) * ra_rows]
        sfx_pages[g] = within + later_pages
        later_pages = later_pages + (within[:, 0:1] + lf_pages[g][:, 0:1])
    sfx_r = jnp.concatenate(sfx_pages, axis=1)
    carry = carry_ref[...]
    m_b, l_b, alpha_b, p_b = _softmax_probs(mb_ref[...], lb_ref[...], s_b + (sfx_r + (fnew + carry)))
    mb_ref[...] = m_b
    lb_ref[...] = l_b
    carry_ref[...] = carry + later_pages

    qi_c = _iota((rc_rows, span), 0) & (nq - 1)
    dist = (qi_c - _iota((rc_rows, span), 1) + (n_pages - first_page) * page).astype(F32)
    m_c, l_c, alpha_c, p_c = _softmax_probs(mc_ref[...], lc_ref[...], s_c - slope_c * dist)
    mc_ref[...] = m_c
    lc_ref[...] = l_c

    accb_ref[...] = alpha_b * accb_ref[...] + _dot_nt(p_b, lanes_cat([r[1] for r in b_refs]))
    accc_ref[...] = alpha_c * accc_ref[...] + _dot(p_c, rows_cat(1))

    @pl.when(j == pl.num_programs(1) - 1)
    def _():
        lane_o = _iota((nq, wa), 1) >> HEAD_SHIFT
        acc_a = acca_ref[...]
        nb = accb_ref[...] / lb_ref[...]
        oa = jnp.zeros((nq, wa), F32)
        ob = jnp.zeros((nq, wa), F32)
        for h in range(N_HEADS):
            oa = jnp.where(lane_o == h, acc_a[h * nq:(h + 1) * nq, :], oa)
            ob = jnp.where(lane_o == h, nb[h * nq:(h + 1) * nq, :], ob)
        oa_ref[...] = oa
        ob_ref[...] = ob
        nc = accc_ref[...] / lc_ref[...]
        for h in range(N_HEADS):
            blk = nc[2 * h * nq:2 * (h + 1) * nq, h * PAIR:(h + 1) * PAIR]
            o = blk[0:nq, :] - lam * blk[nq:2 * nq, :]
            oc_ref[:, h * PAIR:(h + 1) * PAIR] = _rms_rows(o, g_ref[...]) * out_scale


def _sample_mixers(layer, page_table, lam, abf, bbf, cbf, csq, csk, ltri, gsub,
                   pool_a, pool_b, pool_f, pool_c, out_scale):
    dbsz, nq, _ = abf.shape
    n_pages = page_table.shape[1]
    gsz = PAGES_PER_STEP
    page = ltri.shape[0] // gsz
    wa = N_HEADS * HEAD_DIM
    steps = n_pages // gsz

    def seq_spec(w):
        return pl.BlockSpec((None, nq, w), lambda b, j, pt: (b, 0, 0))

    def page_specs(g):
        def idx(b, j, pt):
            return pt[b, n_pages - (j + 1) * gsz + g]
        return [
            pl.BlockSpec((None, None, 2, wa, page), lambda b, j, pt: (layer, idx(b, j, pt), 0, 0, 0)),
            pl.BlockSpec((None, None, N_HEADS, page), lambda b, j, pt: (layer, idx(b, j, pt), 0, 0)),
            pl.BlockSpec((None, None, 2 * N_HEADS * page, PAIR),
                         lambda b, j, pt: (layer, idx(b, j, pt), 0, 0)),
        ]

    in_specs = [pl.BlockSpec(memory_space=pltpu.SMEM), seq_spec(3 * wa), seq_spec(3 * wa), seq_spec(6 * wa),
                seq_spec(LANES), pl.BlockSpec((None, 8, LANES), lambda b, j, pt: (b, 0, 0)),
                pl.BlockSpec(ltri.shape, lambda b, j, pt: (0, 0)),
                pl.BlockSpec((1, PAIR), lambda b, j, pt: (0, 0)),
                pl.BlockSpec(memory_space=pl.ANY)]
    pools = []
    for g in range(gsz):
        in_specs += page_specs(g)
        pools += [pool_b, pool_f, pool_c]
    ra_rows = N_HEADS * nq
    rc_rows = 2 * ra_rows
    scratch = [pltpu.VMEM((ra_rows, 1), F32), pltpu.VMEM((ra_rows, wa), F32),
               pltpu.VMEM((ra_rows, 1), F32), pltpu.VMEM((ra_rows, 1), F32), pltpu.VMEM((ra_rows, wa), F32),
               pltpu.VMEM((ra_rows, 1), F32),
               pltpu.VMEM((rc_rows, 1), F32), pltpu.VMEM((rc_rows, 1), F32), pltpu.VMEM((rc_rows, 2 * wa), F32),
               pltpu.VMEM((gsz, 2, wa, page), F32), pltpu.SemaphoreType.DMA((gsz,))]
    return pl.pallas_call(
        functools.partial(_sample_kernel, layer=layer, pages_per_step=gsz, n_pages=n_pages,
                          out_scale=out_scale),
        grid_spec=pltpu.PrefetchScalarGridSpec(
            num_scalar_prefetch=1,
            grid=(dbsz, steps),
            in_specs=in_specs,
            out_specs=[seq_spec(wa), seq_spec(wa), seq_spec(2 * wa)],
            scratch_shapes=scratch),
        out_shape=[jax.ShapeDtypeStruct((dbsz, nq, wa), F32), jax.ShapeDtypeStruct((dbsz, nq, wa), F32),
                   jax.ShapeDtypeStruct((dbsz, nq, 2 * wa), F32)],
        compiler_params=_params(("arbitrary", "arbitrary")),
        name="sample_mixers",
    )(page_table, lam, abf, bbf, cbf, csq, csk, ltri, gsub, pool_a, *pools)


def _merge_kernel(h_ref, oa_ref, ob_ref, oc_ref, g1_ref, wa_ref, wb_ref, wc_ref, wg_ref, bg_ref, wo_ref,
                  g2_ref, wrh_ref, wrl_ref, br_ref, h2_ref, x2_ref, gates_ref):
    x = h_ref[...]
    d = x.shape[1]
    xb = _rms_rows(x, g1_ref[...]).astype(BF16)
    ys = (_dot(oa_ref[...].astype(BF16), wa_ref[...]),
          _dot(ob_ref[...].astype(BF16), wb_ref[...]),
          _dot(oc_ref[...].astype(BF16), wc_ref[...]))
    merged = jnp.zeros_like(x)
    for idx, y in enumerate(ys):
        gl = _dot(xb, wg_ref[:, idx * d:(idx + 1) * d]) + bg_ref[:, idx * d:(idx + 1) * d]
        merged = merged + y / (1.0 + jnp.exp(-gl))
    h2 = x + _dot(merged.astype(BF16), wo_ref[...])
    h2_ref[...] = h2
    x2 = _rms_rows(h2, g2_ref[...])
    x2_ref[...] = x2.astype(BF16)

    xh, xl = _split(x2)
    wrh = wrh_ref[...]
    logits = _dot(xh, wrh) + _dot(xl, wrh) + _dot(xh, wrl_ref[...]) + br_ref[...]
    lane = _iota((1, LANES), 1)
    lane_f = lane.astype(F32)
    n_exp = N_GROUPS * EXPERTS_PER_GROUP
    is_exp = jnp.logical_and(lane >= GATE_LANE0, lane < GATE_LANE0 + n_exp)
    grp_of_lane = jnp.where(is_exp, ((lane - GATE_LANE0) >> GROUP_SHIFT).astype(F32), -1.0)

    def first_max(vals):
        mx = jnp.max(vals, axis=1, keepdims=True)
        idx = jnp.min(jnp.where(vals == mx, lane_f, float(LANES)), axis=1, keepdims=True)
        return mx, idx

    glog = jnp.where(lane < N_GROUPS, logits, NEG)
    gmax, gidx = first_max(glog)
    p_g = 1.0 / jnp.sum(jnp.exp(glog - gmax), axis=1, keepdims=True)
    elog = jnp.where(grp_of_lane == gidx, logits, NEG)
    m1, i1 = first_max(elog)
    elog2 = jnp.where(lane_f == i1, NEG, elog)
    m2, i2 = first_max(elog2)
    e2 = jnp.exp(m2 - m1)
    w1 = p_g / (1.0 + e2)
    w2 = p_g * e2 / (1.0 + e2)
    gates_ref[...] = jnp.where(lane_f == i1, w1, 0.0) + jnp.where(lane_f == i2, w2, 0.0)


def _merge(h2d, oa, ob, oc, g1, wa, wb, wc, wg, bg, wo, g2, wrh, wrl, br, tm):
    n, d = h2d.shape
    const = lambda a: pl.BlockSpec(a.shape, lambda i: (0,) * a.ndim)
    row = lambda w: pl.BlockSpec((tm, w), lambda i: (i, 0))
    return pl.pallas_call(
        _merge_kernel,
        grid=(n // tm,),
        in_specs=[row(d), row(oa.shape[1]), row(ob.shape[1]), row(oc.shape[1]), const(g1), const(wa),
                  const(wb), const(wc), const(wg), const(bg), const(wo), const(g2), const(wrh),
                  const(wrl), const(br)],
        out_specs=[row(d), row(d), row(LANES)],
        out_shape=[jax.ShapeDtypeStruct((n, d), F32), jax.ShapeDtypeStruct((n, d), BF16),
                   jax.ShapeDtypeStruct((n, LANES), F32)],
        compiler_params=_params(("arbitrary",)),
        name="merge",
    )(h2d, oa, ob, oc, g1, wa, wb, wc, wg, bg, wo, g2, wrh, wrl, br)


def _moe_kernel(x2_ref, gates_ref, h2_ref, wg_ref, wu_ref, wd_ref, o_ref):
    g = pl.program_id(1)
    x = x2_ref[...]
    hg = _dot(x, wg_ref[...])
    hu = _dot(x, wu_ref[...])
    act = hg / (1.0 + jnp.exp(-hg)) * hu
    gates = gates_ref[...]
    lane = _iota((1, LANES), 1)
    parts = []
    for e in range(EXPERTS_PER_GROUP):
        col = jnp.sum(jnp.where(lane == GATE_LANE0 + g * EXPERTS_PER_GROUP + e, gates, 0.0),
                      axis=1, keepdims=True)
        parts.append((act[:, e * D_EXPERT:(e + 1) * D_EXPERT] * col).astype(BF16))
    y = _dot(jnp.concatenate(parts, axis=1), wd_ref[...])

    @pl.when(g == 0)
    def _():
        o_ref[...] = h2_ref[...] + y

    @pl.when(g != 0)
    def _():
        o_ref[...] = o_ref[...] + y


def _moe(x2, gates, h2, wg, wu, wd, tm):
    n, d = h2.shape
    gw = EXPERTS_PER_GROUP * D_EXPERT
    return pl.pallas_call(
        _moe_kernel,
        grid=(n // tm, N_GROUPS),
        in_specs=[pl.BlockSpec((tm, d), lambda i, g: (i, 0)), pl.BlockSpec((tm, LANES), lambda i, g: (i, 0)),
                  pl.BlockSpec((tm, d), lambda i, g: (i, 0)), pl.BlockSpec((d, gw), lambda i, g: (0, g)),
                  pl.BlockSpec((d, gw), lambda i, g: (0, g)), pl.BlockSpec((gw, d), lambda i, g: (g, 0))],
        out_specs=pl.BlockSpec((tm, d), lambda i, g: (i, 0)),
        out_shape=jax.ShapeDtypeStruct((n, d), F32),
        compiler_params=_params(("arbitrary", "arbitrary")),
        name="moe",
    )(x2, gates, h2, wg, wu, wd)


def _pad_lanes(x, lane0=0):
    return jnp.pad(x, ((0, 0), (lane0, LANES - lane0 - x.shape[1])))


def _strict_lower(n):
    return jnp.tril(jnp.ones((n, n), F32), -1).astype(BF16)


def kernel(x_prompt, x_sample, cache_a_kv, cache_b_kv, cache_b_logf, cache_c_kv, page_table, norm1_g, w_in, b_forget, qn_b_g, kn_b_g, qn_c_g, kn_c_g, lam_q1, lam_k1, lam_q2, lam_k2, subln_c_g, w_br_a, w_br_b, w_br_c, w_gate, b_gate, w_out, norm2_g, w_router_grp, b_router_grp, w_router_exp, b_router_exp, w_exp_gate, w_exp_up, w_exp_down):
    bsz, seq, d = x_prompt.shape
    dbsz, nq, _ = x_sample.shape
    depth = w_in.shape[0]
    n_pool, page = cache_a_kv.shape[1], cache_a_kv.shape[2]
    wa = N_HEADS * HEAD_DIM
    n_exp = N_GROUPS * EXPERTS_PER_GROUP
    ns = dbsz * nq
    assert nq == 8 and d % LANES == 0 and seq % TOKEN_TILE == 0
    assert page_table.shape[1] % PAGES_PER_STEP == 0

    pool_a = jnp.transpose(cache_a_kv, (0, 1, 3, 4, 5, 2)).reshape(depth, n_pool, 2, wa, page)
    pool_b = jnp.transpose(cache_b_kv, (0, 1, 3, 4, 5, 2)).reshape(depth, n_pool, 2, wa, page)
    pool_f = jnp.transpose(cache_b_logf, (0, 1, 3, 2))
    pool_c = cache_c_kv.reshape(depth, n_pool, page * 2 * N_HEADS, PAIR)

    seg = (jnp.kron(jnp.eye(N_HEADS, dtype=F32), jnp.ones((HEAD_DIM, HEAD_DIM), F32)) / HEAD_DIM).astype(BF16)
    tri_p = jnp.tril(jnp.ones((TOKEN_TILE, TOKEN_TILE), F32)).astype(BF16)
    r = jnp.arange(ns)
    tri_s = jnp.logical_and(r[:, None] >= r[None, :], (r[:, None] // nq) == (r[None, :] // nq)).astype(BF16)
    ltri_a = _strict_lower(STICK_TILE)
    ltri_pg = _strict_lower(PAGES_PER_STEP * page)
    slopes = jnp.asarray(ALIBI_SLOPES, F32)

    hp = x_prompt.reshape(bsz * seq, d)
    hs = x_sample.reshape(ns, d)
    rows_p = rows_s = None
    lf_p = []
    lf_s = []

    for l in range(depth):
        lam_init = 0.8 - 0.6 * math.exp(-0.3 * l)
        lam = (jnp.exp(jnp.sum(lam_q1[l].astype(F32) * lam_k1[l].astype(F32)))
               - jnp.exp(jnp.sum(lam_q2[l].astype(F32) * lam_k2[l].astype(F32))) + lam_init).reshape(1)
        out_scale = 1.0 - lam_init

        w = w_in[l]
        off_f = 6 * wa
        wab = w[:, :off_f].astype(BF16)
        wcc = w[:, off_f + N_HEADS:].astype(BF16)
        wf =_pad_lanes(w[:, off_f:off_f + N_HEADS]).astype(BF16)
        bfp = _pad_lanes(b_forget[l].reshape(1, N_HEADS).astype(F32))
        g1 = norm1_g[l].reshape(1, d)
        g2 = norm2_g[l].reshape(1, d)
        gb = jnp.stack([jnp.tile(qn_b_g[l], N_HEADS), jnp.tile(kn_b_g[l], N_HEADS)])
        gc = jnp.stack([jnp.tile(qn_c_g[l], 2 * N_HEADS), jnp.tile(kn_c_g[l], 2 * N_HEADS)])
        gsub = subln_c_g[l].reshape(1, PAIR)
        wr = jnp.concatenate([_pad_lanes(w_router_grp[l])[:, :GATE_LANE0],
                              _pad_lanes(w_router_exp[l], GATE_LANE0)[:, GATE_LANE0:]], axis=1)
        wrh = wr.astype(BF16)
        wrl = (wr - wrh.astype(F32)).astype(BF16)
        br = (_pad_lanes(b_router_grp[l].reshape(1, N_GROUPS))
              + _pad_lanes(b_router_exp[l].reshape(1, n_exp), GATE_LANE0))
        weg = jnp.transpose(w_exp_gate[l], (1, 0, 2)).reshape(d, n_exp * D_EXPERT).astype(BF16)
        weu = jnp.transpose(w_exp_up[l], (1, 0, 2)).reshape(d, n_exp * D_EXPERT).astype(BF16)
        wed = w_exp_down[l].reshape(n_exp * D_EXPERT, d).astype(BF16)
        merge_w = (g1, w_br_a[l].astype(BF16), w_br_b[l].astype(BF16), w_br_c[l].astype(BF16),
                   w_gate[l].astype(BF16), b_gate[l].reshape(1, -1), w_out[l].astype(BF16), g2, wrh, wrl, br)
        proj_w = (g1, wab, wcc, wf, bfp, gb, gc, seg)

        *rows_p, lf, cs, abf, bbf, cbf = _proj(hp, tri_p, seq // TOKEN_TILE, l, depth, rows_p, *proj_w)
        lf_p.append(lf[:, :N_HEADS].reshape(bsz, seq, N_HEADS))
        ck = jnp.transpose(cs.reshape(bsz, seq, LANES)[:, :, :8], (0, 2, 1))
        oa, ob, oc = _prompt_mixers(abf.reshape(bsz, seq, -1), bbf.reshape(bsz, seq, -1),
                                    cbf.reshape(bsz, seq, -1), cs.reshape(bsz, seq, LANES), ck,
                                    ltri_a, slopes, lam, gsub, out_scale,
                                    _qk_bound(qn_b_g[l], kn_b_g[l]), _qk_bound(qn_c_g[l], kn_c_g[l]))
        h2, x2, gates = _merge(hp, oa.reshape(bsz * seq, -1), ob.reshape(bsz * seq, -1),
                               oc.reshape(bsz * seq, -1), *merge_w, TOKEN_TILE)
        hp = _moe(x2, gates, h2, weg, weu, wed, TOKEN_TILE)

        *rows_s, lf, cs, abf, bbf, cbf = _proj(hs, tri_s, 1, l, depth, rows_s, *proj_w)
        lf_s.append(lf[:, :N_HEADS].reshape(dbsz, nq, N_HEADS))
        csq = cs.reshape(dbsz, nq, LANES)
        csk = jnp.pad(jnp.transpose(csq[:, :, :8], (0, 2, 1)), ((0, 0), (0, 0), (0, LANES - nq)))
        oa, ob, oc = _sample_mixers(l, page_table, lam, abf.astype(F32).reshape(dbsz, nq, -1),
                                    bbf.astype(F32).reshape(dbsz, nq, -1),
                                    cbf.astype(F32).reshape(dbsz, nq, -1), csq, csk, ltri_pg, gsub,
                                    pool_a, pool_b, pool_f, pool_c, out_scale)
        h2, x2, gates = _merge(hs, oa.reshape(ns, -1), ob.reshape(ns, -1), oc.reshape(ns, -1), *merge_w, ns)
        hs = _moe(x2, gates, h2, weg, weu, wed, ns)

    def cache_rows(rows, lfs, nb, nt):
        a, b, c = rows
        return (a.reshape(depth, nb, nt, 2, N_HEADS, HEAD_DIM), b.reshape(depth, nb, nt, 2, N_HEADS, HEAD_DIM),
                jnp.stack(lfs), c.reshape(depth, nb, nt, 2, N_HEADS, 2 * HEAD_DIM))

    return (hp.reshape(bsz, seq, d), hs.reshape(dbsz, nq, d),
            *cache_rows(rows_p, lf_p, bsz, seq), *cache_rows(rows_s, lf_s, dbsz, nq))
```
